```python
import jax, jax.numpy as jnp
from jax import lax
import numpy as np

D_MODEL = 1024
BATCH = 8
SEQ = 8192
DEPTH = 4

HEAD_DIM = 64
SB_WIDTH = D_MODEL // 2
SB_HEADS = SB_WIDTH // HEAD_DIM
Q_BLOCK = 128
SB_WINDOW = 512
KV_SPAN = SB_WINDOW + Q_BLOCK
POOL_WIDTH = D_MODEL // 4
POOL_WINDOWS = (2, 4, 8, 16)
POOL_GROUPS = len(POOL_WINDOWS)
POOL_GROUP_DIM = POOL_WIDTH // POOL_GROUPS
SG_WIDTH = D_MODEL // 4
SG_HEADS = 4
SG_HEAD_DIM = SG_WIDTH // SG_HEADS
CHUNK = 128
MIX_WIDTH = SB_WIDTH + POOL_WIDTH + SG_WIDTH
IN_WIDTH = 3 * SB_WIDTH + POOL_WIDTH + 2 * SG_WIDTH
SPLITS = tuple(int(i) for i in np.cumsum([SB_WIDTH, SB_WIDTH, SB_WIDTH, POOL_WIDTH, SG_WIDTH]))
D_FF = -(-8 * D_MODEL // (3 * 256)) * 256
N_MOD = 6
EPS = 1e-6

kernel_name = 'hybrid_sb_pool_sgmlp_adaln_trunk'


def rmsnorm(x, g):
    xf = x.astype(jnp.float32)
    y = xf * lax.rsqrt(jnp.mean(xf * xf, axis=-1, keepdims=True) + EPS)
    return (y * g.astype(jnp.float32)).astype(x.dtype)


def layernorm(x, g):
    xf = x.astype(jnp.float32)
    mu = jnp.mean(xf, axis=-1, keepdims=True)
    var = jnp.mean(jnp.square(xf - mu), axis=-1, keepdims=True)
    return ((xf - mu) * lax.rsqrt(var + EPS) * g.astype(jnp.float32)).astype(x.dtype)


def stick_breaking_attention(q, k, v):
    b, s, h, d = q.shape
    nb = s // Q_BLOCK
    qb = q.reshape(b, nb, Q_BLOCK, h, d).transpose(1, 0, 2, 3, 4)
    pad = ((0, 0), (SB_WINDOW, 0), (0, 0), (0, 0))
    kp = jnp.pad(k.astype(jnp.float32), pad)
    vp = jnp.pad(v.astype(jnp.float32), pad)
    scale = d ** -0.5

    def one_block(args):
        q_blk, i = args
        start = i * Q_BLOCK
        k_blk = lax.dynamic_slice_in_dim(kp, start, KV_SPAN, axis=1)
        v_blk = lax.dynamic_slice_in_dim(vp, start, KV_SPAN, axis=1)
        q_pos = i * Q_BLOCK + jnp.arange(Q_BLOCK)
        key_pos = i * Q_BLOCK - SB_WINDOW + jnp.arange(KV_SPAN)
        valid = ((key_pos[None, :] < q_pos[:, None])
                 & (key_pos[None, :] >= q_pos[:, None] - SB_WINDOW)
                 & (key_pos[None, :] >= 0))
        z = jnp.einsum('bqhd,bkhd->bhqk', q_blk.astype(jnp.float32), k_blk) * scale
        log_rest = jnp.where(valid, jax.nn.log_sigmoid(-z), 0.0)
        later = lax.cumsum(log_rest, axis=3, reverse=True) - log_rest
        w = jnp.where(valid, jnp.exp(log_rest + z + later), 0.0)
        return jnp.einsum('bhqk,bkhd->bqhd', w, v_blk)

    out = lax.map(one_block, (qb, jnp.arange(nb)))
    return out.transpose(1, 0, 2, 3, 4).reshape(b, s, h * d).astype(q.dtype)


def multiscale_pool(p, w_pool, pool_scale):
    b, s, _ = p.shape
    pg = p.astype(jnp.float32).reshape(b, s, POOL_GROUPS, POOL_GROUP_DIM)
    maxw = max(POOL_WINDOWS)
    csp = jnp.pad(jnp.cumsum(pg, axis=1), ((0, 0), (maxw, 0), (0, 0), (0, 0)))
    pos = jnp.arange(s)
    outs = []
    for g, w in enumerate(POOL_WINDOWS):
        window_sum = csp[:, maxw:, g] - csp[:, maxw - w:maxw - w + s, g]
        count = jnp.minimum(pos + 1, w).astype(jnp.float32)[None, :, None]
        outs.append(window_sum / count - pg[:, :, g])
    pooled = jnp.stack(outs, axis=2)
    mixed = jnp.einsum('bsgc,gce->bsge', pooled, w_pool.astype(jnp.float32))
    return (mixed.reshape(b, s, POOL_WIDTH) * pool_scale).astype(p.dtype)


def chunked_spatial_gating(u, vg, sg_norm, w_s, b_s):
    b, s, _ = u.shape
    n = s // CHUNK
    u = jax.nn.gelu(u)
    vn = layernorm(jax.nn.gelu(vg), sg_norm)
    vh = vn.reshape(b, n, CHUNK, SG_HEADS, SG_HEAD_DIM)
    causal = jnp.tril(jnp.ones((CHUNK, CHUNK), dtype=bool))
    ws = jnp.where(causal[None], w_s, 0.0)
    mixed = jnp.einsum('hts,bnshd->bnthd', ws, vh) + b_s.T[None, None, :, :, None]
    return u * mixed.reshape(b, s, SG_WIDTH)


def setup_inputs(seed: int = 0) -> dict:
    key = jax.random.key(seed)
    ks = jax.random.split(key, 20)
    f32 = jnp.float32
    nrm = lambda k, shape, s: jax.random.normal(k, shape, f32) * s
    return {
        'x': nrm(ks[0], (BATCH, SEQ, D_MODEL), 1.0),
        'c': nrm(ks[1], (BATCH, D_MODEL), 1.0),
        'w_ada': nrm(ks[2], (DEPTH, D_MODEL, N_MOD * D_MODEL), 0.5 * D_MODEL ** -0.5),
        'b_ada': nrm(ks[3], (DEPTH, N_MOD * D_MODEL), 0.02),
        'norm_mix_in': 1.0 + nrm(ks[4], (DEPTH, D_MODEL), 0.02),
        'w_in': nrm(ks[5], (DEPTH, D_MODEL, IN_WIDTH), D_MODEL ** -0.5),
        'w_pool': nrm(ks[6], (DEPTH, POOL_GROUPS, POOL_GROUP_DIM, POOL_GROUP_DIM), POOL_GROUP_DIM ** -0.5),
        'pool_scale': 1.0 + nrm(ks[7], (DEPTH, POOL_WIDTH), 0.1),
        'sg_norm': 1.0 + nrm(ks[8], (DEPTH, SG_WIDTH), 0.02),
        'w_s': nrm(ks[9], (DEPTH, SG_HEADS, CHUNK, CHUNK), CHUNK ** -0.5),
        'b_s': 1.0 + nrm(ks[10], (DEPTH, SG_HEADS, CHUNK), 0.02),
        'mix_norm': 1.0 + nrm(ks[11], (DEPTH, MIX_WIDTH), 0.02),
        'w_out': nrm(ks[12], (DEPTH, MIX_WIDTH, D_MODEL), MIX_WIDTH ** -0.5),
        'norm_ffn_in': 1.0 + nrm(ks[13], (DEPTH, D_MODEL), 0.02),
        'w_gate_up': nrm(ks[14], (DEPTH, D_MODEL, 2 * D_FF), D_MODEL ** -0.5),
        'w_down': nrm(ks[15], (DEPTH, D_FF, D_MODEL), D_FF ** -0.5),
        'final_norm': 1.0 + nrm(ks[16], (D_MODEL,), 0.02),
    }


def reference(x, c, w_ada, b_ada, norm_mix_in, w_in, w_pool, pool_scale, sg_norm, w_s, b_s,
              mix_norm, w_out, norm_ffn_in, w_gate_up, w_down, final_norm):
    b, s, _ = x.shape
    c_act = jax.nn.silu(c)
    for l in range(DEPTH):
        mod = (c_act @ w_ada[l] + b_ada[l])[:, None, :]
        sh1, sc1, g1, sh2, sc2, g2 = jnp.split(mod, N_MOD, axis=-1)

        h = rmsnorm(x, norm_mix_in[l]) * (1.0 + sc1) + sh1
        proj = h @ w_in[l]
        q, k, v, p, u, vg = jnp.split(proj, SPLITS, axis=-1)
        a_out = stick_breaking_attention(q.reshape(b, s, SB_HEADS, HEAD_DIM),
                                         k.reshape(b, s, SB_HEADS, HEAD_DIM),
                                         v.reshape(b, s, SB_HEADS, HEAD_DIM))
        p_out = multiscale_pool(p, w_pool[l], pool_scale[l])
        g_out = chunked_spatial_gating(u, vg, sg_norm[l], w_s[l], b_s[l])
        mn = mix_norm[l]
        merged = jnp.concatenate([
            rmsnorm(a_out, mn[:SB_WIDTH]),
            rmsnorm(p_out, mn[SB_WIDTH:SB_WIDTH + POOL_WIDTH]),
            rmsnorm(g_out, mn[SB_WIDTH + POOL_WIDTH:]),
        ], axis=-1)
        x = x + g1 * (merged @ w_out[l])

        h2 = rmsnorm(x, norm_ffn_in[l]) * (1.0 + sc2) + sh2
        gate, up = jnp.split(h2 @ w_gate_up[l], 2, axis=-1)
        x = x + g2 * ((jax.nn.silu(gate) * up) @ w_down[l])
    return rmsnorm(x, final_norm)
```

```python
import functools

import numpy as np
import jax
import jax.numpy as jnp
from jax import lax
from jax.experimental import pallas as pl
from jax.experimental.pallas import tpu as pltpu

F32 = jnp.float32
BF16 = jnp.bfloat16

HEAD_DIM = 64
Q_BLOCK = 128
SB_WINDOW = 512
KV_SPAN = SB_WINDOW + Q_BLOCK
POOL_WINDOWS = (2, 4, 8, 16)
POOL_HALO = 16
CHUNK = 128
SG_HEADS = 4
N_MOD = 6
EPS = 1e-6
LANES = 128
NEG_BIG = -1e30

ROW_TILE = 512
VMEM_LIMIT = 56 * 1024 * 1024


def _rms(x, gain):
    return x * lax.rsqrt(jnp.mean(x * x, axis=-1, keepdims=True) + EPS) * gain


def _ada_kernel(c_ref, w_ref, b_ref, o_ref):
    c = c_ref[...]
    c_act = c / (1.0 + jnp.exp(-c))
    o_ref[0] = jnp.dot(c_act, w_ref[0], preferred_element_type=F32,
                       precision=lax.Precision.HIGHEST) + b_ref[0]


def _ada(c, w_ada, b_ada):
    depth, d, nd = w_ada.shape
    b = c.shape[0]
    return pl.pallas_call(
        _ada_kernel,
        grid=(depth, nd // d),
        in_specs=[
            pl.BlockSpec((b, d), lambda l, j: (0, 0)),
            pl.BlockSpec((1, d, d), lambda l, j: (l, 0, j)),
            pl.BlockSpec((1, 1, d), lambda l, j: (l, 0, j)),
        ],
        out_specs=pl.BlockSpec((1, b, d), lambda l, j: (l, 0, j)),
        out_shape=jax.ShapeDtypeStruct((depth, b, nd), F32),
        name="ada_mod",
    )(c, w_ada, b_ada.reshape(depth, 1, nd))


def _front_kernel(x_ref, mod_ref, nrm_ref, w_in_ref, wpool_ref, pscale_ref, sgn_ref, wcat_ref,
                  sbias_ref, mnp_ref, mng_ref, q_ref, k_ref, v_ref, pg_ref, e_ref, *, tm, sbw, pw):
    s = pl.program_id(1)
    x = x_ref[0]
    h = _rms(x, nrm_ref[...]) * (1.0 + mod_ref[0, 1:2, :]) + mod_ref[0, 0:1, :]
    hb = h.astype(BF16)

    npair = sbw // LANES
    for t, ref in enumerate((q_ref, k_ref, v_ref)):
        y = jnp.dot(hb, w_in_ref[:, t * sbw:(t + 1) * sbw], preferred_element_type=F32)
        for j in range(npair):
            ref[0, j] = y[:, j * LANES:(j + 1) * LANES].astype(BF16)

    c0 = 3 * sbw
    p = jnp.dot(hb, w_in_ref[:, c0:c0 + pw], preferred_element_type=F32)

    @pl.when(s == 0)
    def _():
        e_ref[0:POOL_HALO, :] = jnp.zeros((POOL_HALO, pw), F32)

    @pl.when(s > 0)
    def _():
        e_ref[0:POOL_HALO, :] = e_ref[tm:tm + POOL_HALO, :]

    e_ref[POOL_HALO:, :] = p
    e = e_ref[...]
    s2 = e + pltpu.roll(e, 1, 0)
    s4 = s2 + pltpu.roll(s2, 2, 0)
    s8 = s4 + pltpu.roll(s4, 4, 0)
    s16 = s8 + pltpu.roll(s8, 8, 0)
    gdim = pw // len(POOL_WINDOWS)
    lane_e = lax.broadcasted_iota(jnp.int32, (tm + POOL_HALO, pw), 1)
    wsum = jnp.where(lane_e < gdim, s2,
                     jnp.where(lane_e < 2 * gdim, s4, jnp.where(lane_e < 3 * gdim, s8, s16)))
    wsum = wsum[POOL_HALO:, :]
    lane = lax.broadcasted_iota(jnp.int32, (tm, pw), 1)
    pos = s * tm + lax.broadcasted_iota(jnp.int32, (tm, pw), 0)
    wlen = jnp.where(lane < gdim, POOL_WINDOWS[0],
                     jnp.where(lane < 2 * gdim, POOL_WINDOWS[1],
                               jnp.where(lane < 3 * gdim, POOL_WINDOWS[2], POOL_WINDOWS[3])))
    count = jnp.minimum(pos + 1, wlen).astype(F32)
    pooled = wsum / count - p
    mixed = jnp.dot(pooled.astype(BF16), wpool_ref[...], preferred_element_type=F32) * pscale_ref[...]
    pg_ref[0, :, 0:pw] = _rms(mixed, mnp_ref[...]).astype(BF16)

    c1 = c0 + pw
    u = jnp.dot(hb, w_in_ref[:, c1:c1 + pw], preferred_element_type=F32)
    vg = jnp.dot(hb, w_in_ref[:, c1 + pw:c1 + 2 * pw], preferred_element_type=F32)
    ug = jax.nn.gelu(u)
    vge = jax.nn.gelu(vg)
    mu = jnp.mean(vge, axis=-1, keepdims=True)
    var = jnp.mean(jnp.square(vge - mu), axis=-1, keepdims=True)
    vn = ((vge - mu) * lax.rsqrt(var + EPS) * sgn_ref[...]).astype(BF16)
    hd = pw // SG_HEADS
    lane_c = lax.broadcasted_iota(jnp.int32, (CHUNK, pw), 1)
    zero = jnp.zeros((CHUNK, pw), BF16)
    outs = []
    for c in range(tm // CHUNK):
        vc = vn[c * CHUNK:(c + 1) * CHUNK, :]
        stacked = jnp.concatenate(
            [jnp.where((lane_c >= hh * hd) & (lane_c < (hh + 1) * hd), vc, zero) for hh in range(SG_HEADS)],
            axis=0)
        outs.append(jnp.dot(wcat_ref[...], stacked, preferred_element_type=F32) + sbias_ref[...])
    g_out = ug * jnp.concatenate(outs, axis=0)
    pg_ref[0, :, pw:2 * pw] = _rms(g_out, mng_ref[...]).astype(BF16)


def _front(x, mod, nrm, w_in_b, wpool_bd, pscale, sgn, wcat, sbias, mnp, mng, *, tm, sbw, pw):
    b, s, d = x.shape
    npair = sbw // LANES
    in_w = w_in_b.shape[1]
    const = lambda *shape: pl.BlockSpec(shape, lambda bi, si: (0,) * len(shape))
    qkv_spec = pl.BlockSpec((1, npair, tm, LANES), lambda bi, si: (bi, 0, si, 0))
    qkv_shape = jax.ShapeDtypeStruct((b, npair, s, LANES), BF16)
    return pl.pallas_call(
        functools.partial(_front_kernel, tm=tm, sbw=sbw, pw=pw),
        grid=(b, s // tm),
        in_specs=[
            pl.BlockSpec((1, tm, d), lambda bi, si: (bi, si, 0)),
            pl.BlockSpec((1, N_MOD, d), lambda bi, si: (bi, 0, 0)),
            const(1, d), const(d, in_w), const(pw, pw), const(1, pw), const(1, pw),
            const(CHUNK, SG_HEADS * CHUNK), const(CHUNK, pw), const(1, pw), const(1, pw),
        ],
        out_specs=[qkv_spec, qkv_spec, qkv_spec,
                   pl.BlockSpec((1, tm, 2 * pw), lambda bi, si: (bi, si, 0))],
        out_shape=[qkv_shape, qkv_shape, qkv_shape, jax.ShapeDtypeStruct((b, s, 2 * pw), BF16)],
        scratch_shapes=[pltpu.VMEM((tm + POOL_HALO, pw), F32)],
        compiler_params=pltpu.CompilerParams(
            dimension_semantics=("arbitrary", "arbitrary"), vmem_limit_bytes=VMEM_LIMIT),
        name="front",
    )(x, mod, nrm, w_in_b, wpool_bd, pscale, sgn, wcat, sbias, mnp, mng)


def _attn_kernel(q_ref, kp_ref, kc_ref, vp_ref, vc_ref, pg_ref, x_ref, mod_ref, bias_ref, tri_ref,
                 mna_ref, wout_ref, o_ref, kwin, vwin, a_scr, *, tq, npair):
    i = pl.program_id(1)
    kwin[:, 0:tq, :] = kp_ref[0]
    kwin[:, tq:, :] = kc_ref[0]
    vwin[:, 0:tq, :] = vp_ref[0]
    vwin[:, tq:, :] = vc_ref[0]
    nqb = tq // Q_BLOCK
    lane_q = lax.broadcasted_iota(jnp.int32, (Q_BLOCK, LANES), 1)
    lane_k = lax.broadcasted_iota(jnp.int32, (KV_SPAN, LANES), 1)

    def body(t, carry):
        qb = t // npair
        pr = t % npair
        off = pl.multiple_of(qb * Q_BLOCK, Q_BLOCK)
        q2 = q_ref[0, pr, pl.ds(off, Q_BLOCK), :]
        k2 = kwin[pr, pl.ds(off, KV_SPAN), :]
        v2 = vwin[pr, pl.ds(off, KV_SPAN), :]
        bias = bias_ref[jnp.where(i == 0, qb + 1, 0)]
        acc = jnp.zeros((Q_BLOCK, LANES), F32)
        for hh in range(LANES // HEAD_DIM):
            lo, hi = hh * HEAD_DIM, (hh + 1) * HEAD_DIM
            qm = jnp.where((lane_q >= lo) & (lane_q < hi), q2, jnp.zeros_like(q2))
            z = lax.dot_general(qm, k2, (((1,), (1,)), ((), ())), preferred_element_type=F32) + bias
            n = jnp.maximum(z, 0.0) + jnp.log1p(jnp.exp(-jnp.abs(z)))
            later = jnp.dot(n.astype(BF16), tri_ref[...], preferred_element_type=F32)
            w = jnp.exp(z - n - later).astype(BF16)
            vm = jnp.where((lane_k >= lo) & (lane_k < hi), v2, jnp.zeros_like(v2))
            acc = acc + jnp.dot(w, vm, preferred_element_type=F32)
        a_scr[pr, pl.ds(off, Q_BLOCK), :] = acc
        return carry

    lax.fori_loop(0, nqb * npair, body, 0)

    a = jnp.concatenate([a_scr[j] for j in range(npair)], axis=1)
    an = _rms(a, mna_ref[...]).astype(BF16)
    merged = jnp.concatenate([an, pg_ref[0]], axis=1)
    y = jnp.dot(merged, wout_ref[...], preferred_element_type=F32)
    o_ref[0] = x_ref[0] + mod_ref[0, 2:3, :] * y


def _attn(q, k, v, pg, x, mod, bias_tbl, tri, mna, wout_b, *, tq):
    b, s, d = x.shape
    npair = q.shape[1]
    mixw = wout_b.shape[0]
    const = lambda *shape: pl.BlockSpec(shape, lambda bi, si: (0,) * len(shape))
    cur = pl.BlockSpec((1, npair, tq, LANES), lambda bi, si: (bi, 0, si, 0))
    prev = pl.BlockSpec((1, npair, tq, LANES), lambda bi, si: (bi, 0, jnp.maximum(si - 1, 0), 0))
    return pl.pallas_call(
        functools.partial(_attn_kernel, tq=tq, npair=npair),
        grid=(b, s // tq),
        in_specs=[
            cur, prev, cur, prev, cur,
            pl.BlockSpec((1, tq, pg.shape[2]), lambda bi, si: (bi, si, 0)),
            pl.BlockSpec((1, tq, d), lambda bi, si: (bi, si, 0)),
            pl.BlockSpec((1, N_MOD, d), lambda bi, si: (bi, 0, 0)),
            const(*bias_tbl.shape), const(KV_SPAN, KV_SPAN), const(1, npair * LANES), const(mixw, d),
        ],
        out_specs=pl.BlockSpec((1, tq, d), lambda bi, si: (bi, si, 0)),
        out_shape=jax.ShapeDtypeStruct((b, s, d), F32),
        scratch_shapes=[pltpu.VMEM((npair, 2 * tq, LANES), BF16),
                        pltpu.VMEM((npair, 2 * tq, LANES), BF16),
                        pltpu.VMEM((npair, tq, LANES), F32)],
        compiler_params=pltpu.CompilerParams(
            dimension_semantics=("arbitrary", "arbitrary"), vmem_limit_bytes=VMEM_LIMIT),
        name="attn",
    )(q, k, k, v, v, pg, x, mod, bias_tbl, tri, mna, wout_b)


def _ffn_kernel(x_ref, mod_ref, nrm_ref, wg_ref, wu_ref, wd_ref, fn_ref, o_ref, *, final):
    x = x_ref[0]
    h = _rms(x, nrm_ref[...]) * (1.0 + mod_ref[0, 4:5, :]) + mod_ref[0, 3:4, :]
    hb = h.astype(BF16)
    gate = jnp.dot(hb, wg_ref[...], preferred_element_type=F32)
    up = jnp.dot(hb, wu_ref[...], preferred_element_type=F32)
    act = (gate / (1.0 + jnp.exp(-gate)) * up).astype(BF16)
    y = jnp.dot(act, wd_ref[...], preferred_element_type=F32)
    out = x + mod_ref[0, 5:6, :] * y
    if final:
        out = _rms(out, fn_ref[...])
    o_ref[0] = out


def _ffn(x, mod, nrm, wg_b, wu_b, wd_b, fn, *, tm, final):
    b, s, d = x.shape
    dff = wg_b.shape[1]
    once = lambda *shape: pl.BlockSpec(shape, lambda bi, si: (0,) * len(shape),
                                       pipeline_mode=pl.Buffered(1))
    return pl.pallas_call(
        functools.partial(_ffn_kernel, final=final),
        grid=(b, s // tm),
        in_specs=[
            pl.BlockSpec((1, tm, d), lambda bi, si: (bi, si, 0)),
            pl.BlockSpec((1, N_MOD, d), lambda bi, si: (bi, 0, 0)),
            once(1, d), once(d, dff), once(d, dff), once(dff, d), once(1, d),
        ],
        out_specs=pl.BlockSpec((1, tm, d), lambda bi, si: (bi, si, 0)),
        out_shape=jax.ShapeDtypeStruct((b, s, d), F32),
        compiler_params=pltpu.CompilerParams(
            dimension_semantics=("arbitrary", "arbitrary"), vmem_limit_bytes=VMEM_LIMIT),
        name="ffn",
    )(x, mod, nrm, wg_b, wu_b, wd_b, fn)


def _attn_bias_table(tq):
    a = np.arange(Q_BLOCK)[:, None]
    j = np.arange(KV_SPAN)[None, :]
    window = (j >= a) & (j < a + SB_WINDOW)
    rows = [window]
    for qb in range(tq // Q_BLOCK):
        rows.append(window & (j >= SB_WINDOW - qb * Q_BLOCK))
    return np.where(np.stack(rows), 0.0, NEG_BIG).astype(np.float32)


def _suffix_sum_matrix():
    j = np.arange(KV_SPAN)
    return (j[:, None] > j[None, :]).astype(np.float32)


def kernel(x, c, w_ada, b_ada, norm_mix_in, w_in, w_pool, pool_scale, sg_norm, w_s, b_s, mix_norm, w_out,
           norm_ffn_in, w_gate_up, w_down, final_norm):
    b, s, d = x.shape
    depth = w_ada.shape[0]
    pw = pool_scale.shape[1]
    sbw = (w_in.shape[2] - 3 * pw) // 3
    dff = w_down.shape[1]
    tm = min(ROW_TILE, s)
    assert s % tm == 0 and tm % Q_BLOCK == 0 and tm >= SB_WINDOW and sbw % LANES == 0

    mod_all = _ada(c, w_ada, b_ada).reshape(depth, b, N_MOD, d)

    q_scale = jnp.concatenate([jnp.full((sbw,), HEAD_DIM ** -0.5, F32), jnp.ones((w_in.shape[2] - sbw,), F32)])
    w_in_b = (w_in * q_scale).astype(BF16)
    groups = w_pool.shape[1]
    gdim = w_pool.shape[2]
    eye = jnp.eye(groups, dtype=F32)
    wpool_bd = (w_pool[:, :, :, None, :] * eye[None, :, None, :, None]).reshape(
        depth, groups * gdim, groups * gdim).astype(BF16)
    causal = jnp.tril(jnp.ones((CHUNK, CHUNK), dtype=bool))
    ws_c = jnp.where(causal[None, None], w_s, 0.0)
    wcat = ws_c.transpose(0, 2, 1, 3).reshape(depth, CHUNK, SG_HEADS * CHUNK).astype(BF16)
    sbias = jnp.repeat(b_s.transpose(0, 2, 1), pw // SG_HEADS, axis=2)
    w_out_b = w_out.astype(BF16)
    wg_b = w_gate_up[:, :, :dff].astype(BF16)
    wu_b = w_gate_up[:, :, dff:].astype(BF16)
    wd_b = w_down.astype(BF16)
    bias_tbl = jnp.asarray(_attn_bias_table(tm))
    tri = jnp.asarray(_suffix_sum_matrix(), dtype=BF16)
    row = lambda v: v.reshape(1, -1)

    for l in range(depth):
        mod = mod_all[l]
        q, k, v, pg = _front(x, mod, row(norm_mix_in[l]), w_in_b[l], wpool_bd[l], row(pool_scale[l]),
                             row(sg_norm[l]), wcat[l], sbias[l], row(mix_norm[l, sbw:sbw + pw]),
                             row(mix_norm[l, sbw + pw:]), tm=tm, sbw=sbw, pw=pw)
        x = _attn(q, k, v, pg, x, mod, bias_tbl, tri, row(mix_norm[l, :sbw]), w_out_b[l], tq=tm)
        x = _ffn(x, mod, row(norm_ffn_in[l]), wg_b[l], wu_b[l], wd_b[l], row(final_norm),
                 tm=tm, final=(l == depth - 1))
    return x
```

```python
import functools

import numpy as np
import jax
import jax.numpy as jnp
from jax import lax
from jax.experimental import pallas as pl
from jax.experimental.pallas import tpu as pltpu

F32 = jnp.float32
BF16 = jnp.bfloat16

HEAD_DIM = 64
Q_BLOCK = 128
SB_WINDOW = 512
KV_SPAN = SB_WINDOW + Q_BLOCK
POOL_WINDOWS = (2, 4, 8, 16)
POOL_HALO = 16
CHUNK = 128
SG_HEADS = 4
N_MOD = 6
EPS = 1e-6
LANES = 128
NEG_BIG = -1e30
LOG2E = 1.4426950408889634

ROW_TILE = 512
VMEM_LIMIT = 56 * 1024 * 1024


def _rms(x, gain):
    return x * lax.rsqrt(jnp.mean(x * x, axis=-1, keepdims=True) + EPS) * gain


def _ada_kernel(c_ref, w_ref, b_ref, o_ref):
    c = c_ref[...]
    c_act = c / (1.0 + jnp.exp(-c))
    o_ref[0] = jnp.dot(c_act, w_ref[0], preferred_element_type=F32,
                       precision=lax.Precision.HIGHEST) + b_ref[0]


def _ada(c, w_ada, b_ada):
    depth, d, nd = w_ada.shape
    b = c.shape[0]
    return pl.pallas_call(
        _ada_kernel,
        grid=(depth, nd // d),
        in_specs=[
            pl.BlockSpec((b, d), lambda l, j: (0, 0)),
            pl.BlockSpec((1, d, d), lambda l, j: (l, 0, j)),
            pl.BlockSpec((1, 1, d), lambda l, j: (l, 0, j)),
        ],
        out_specs=pl.BlockSpec((1, b, d), lambda l, j: (l, 0, j)),
        out_shape=jax.ShapeDtypeStruct((depth, b, nd), F32),
        name="ada_mod",
    )(c, w_ada, b_ada.reshape(depth, 1, nd))


def _front_kernel(x_ref, mod_ref, nrm_ref, w_in_ref, wpool_ref, pscale_ref, sgn_ref, wcat_ref,
                  sbias_ref, mnp_ref, mng_ref, q_ref, k_ref, v_ref, pg_ref, e_ref, *, tm, sbw, pw):
    s = pl.program_id(1)
    x = x_ref[0]
    h = _rms(x, nrm_ref[...]) * (1.0 + mod_ref[0, 1:2, :]) + mod_ref[0, 0:1, :]
    hb = h.astype(BF16)

    npair = sbw // LANES
    for t, ref in enumerate((q_ref, k_ref, v_ref)):
        y = jnp.dot(hb, w_in_ref[:, t * sbw:(t + 1) * sbw], preferred_element_type=F32)
        for j in range(npair):
            ref[0, j] = y[:, j * LANES:(j + 1) * LANES].astype(BF16)

    c0 = 3 * sbw
    p = jnp.dot(hb, w_in_ref[:, c0:c0 + pw], preferred_element_type=F32)

    @pl.when(s == 0)
    def _():
        e_ref[0:POOL_HALO, :] = jnp.zeros((POOL_HALO, pw), F32)

    @pl.when(s > 0)
    def _():
        e_ref[0:POOL_HALO, :] = e_ref[tm:tm + POOL_HALO, :]

    e_ref[POOL_HALO:, :] = p
    e = e_ref[...]
    s2 = e + pltpu.roll(e, 1, 0)
    s4 = s2 + pltpu.roll(s2, 2, 0)
    s8 = s4 + pltpu.roll(s4, 4, 0)
    s16 = s8 + pltpu.roll(s8, 8, 0)
    gdim = pw // len(POOL_WINDOWS)
    lane_e = lax.broadcasted_iota(jnp.int32, (tm + POOL_HALO, pw), 1)
    wsum = jnp.where(lane_e < gdim, s2,
                     jnp.where(lane_e < 2 * gdim, s4, jnp.where(lane_e < 3 * gdim, s8, s16)))
    wsum = wsum[POOL_HALO:, :]
    lane = lax.broadcasted_iota(jnp.int32, (tm, pw), 1)
    pos = s * tm + lax.broadcasted_iota(jnp.int32, (tm, pw), 0)
    wlen = jnp.where(lane < gdim, POOL_WINDOWS[0],
                     jnp.where(lane < 2 * gdim, POOL_WINDOWS[1],
                               jnp.where(lane < 3 * gdim, POOL_WINDOWS[2], POOL_WINDOWS[3])))
    count = jnp.minimum(pos + 1, wlen).astype(F32)
    pooled = wsum / count - p
    mixed = jnp.dot(pooled.astype(BF16), wpool_ref[...], preferred_element_type=F32) * pscale_ref[...]
    pg_ref[0, :, 0:pw] = _rms(mixed, mnp_ref[...]).astype(BF16)

    c1 = c0 + pw
    u = jnp.dot(hb, w_in_ref[:, c1:c1 + pw], preferred_element_type=F32)
    vg = jnp.dot(hb, w_in_ref[:, c1 + pw:c1 + 2 * pw], preferred_element_type=F32)
    ug = jax.nn.gelu(u)
    vge = jax.nn.gelu(vg)
    mu = jnp.mean(vge, axis=-1, keepdims=True)
    var = jnp.mean(jnp.square(vge - mu), axis=-1, keepdims=True)
    vn = ((vge - mu) * lax.rsqrt(var + EPS) * sgn_ref[...]).astype(BF16)
    hd = pw // SG_HEADS
    lane_c = lax.broadcasted_iota(jnp.int32, (CHUNK, pw), 1)
    zero = jnp.zeros((CHUNK, pw), BF16)
    outs = []
    for c in range(tm // CHUNK):
        vc = vn[c * CHUNK:(c + 1) * CHUNK, :]
        stacked = jnp.concatenate(
            [jnp.where((lane_c >= hh * hd) & (lane_c < (hh + 1) * hd), vc, zero) for hh in range(SG_HEADS)],
            axis=0)
        outs.append(jnp.dot(wcat_ref[...], stacked, preferred_element_type=F32) + sbias_ref[...])
    g_out = ug * jnp.concatenate(outs, axis=0)
    pg_ref[0, :, pw:2 * pw] = _rms(g_out, mng_ref[...]).astype(BF16)


def _front(x, mod, nrm, w_in_b, wpool_bd, pscale, sgn, wcat, sbias, mnp, mng, *, tm, sbw, pw):
    b, s, d = x.shape
    npair = sbw // LANES
    in_w = w_in_b.shape[1]
    const = lambda *shape: pl.BlockSpec(shape, lambda bi, si: (0,) * len(shape))
    qkv_spec = pl.BlockSpec((1, npair, tm, LANES), lambda bi, si: (bi, 0, si, 0))
    qkv_shape = jax.ShapeDtypeStruct((b, npair, s, LANES), BF16)
    return pl.pallas_call(
        functools.partial(_front_kernel, tm=tm, sbw=sbw, pw=pw),
        grid=(b, s // tm),
        in_specs=[
            pl.BlockSpec((1, tm, d), lambda bi, si: (bi, si, 0)),
            pl.BlockSpec((1, N_MOD, d), lambda bi, si: (bi, 0, 0)),
            const(1, d), const(d, in_w), const(pw, pw), const(1, pw), const(1, pw),
            const(CHUNK, SG_HEADS * CHUNK), const(CHUNK, pw), const(1, pw), const(1, pw),
        ],
        out_specs=[qkv_spec, qkv_spec, qkv_spec,
                   pl.BlockSpec((1, tm, 2 * pw), lambda bi, si: (bi, si, 0))],
        out_shape=[qkv_shape, qkv_shape, qkv_shape, jax.ShapeDtypeStruct((b, s, 2 * pw), BF16)],
        scratch_shapes=[pltpu.VMEM((tm + POOL_HALO, pw), F32)],
        compiler_params=pltpu.CompilerParams(
            dimension_semantics=("arbitrary", "arbitrary"), vmem_limit_bytes=VMEM_LIMIT),
        name="front",
    )(x, mod, nrm, w_in_b, wpool_bd, pscale, sgn, wcat, sbias, mnp, mng)


KEY_CHUNKS = ((512, 128), (256, 256), (0, 256))


def _neg_abs(z):
    sign = jnp.uint32(0x80000000)
    return lax.bitcast_convert_type(lax.bitcast_convert_type(z, jnp.uint32) | sign, F32)


def _attn_kernel(q_ref, kp_ref, kc_ref, vp_ref, vc_ref, pg_ref, x_ref, mod_ref, lc_ref, blo_ref, bhi_ref,
                 tri_ref, mna_ref, wout_ref, o_ref, kwin, vwin, a_scr, *, tq, npair):
    i = pl.program_id(1)
    nqb = tq // Q_BLOCK
    keep = (lc_ref[0:1, :], lc_ref[1:2, :])
    spare = (lc_ref[2:3, :], lc_ref[3:4, :])
    kb_prev = jnp.where(i == 0, NEG_BIG, 0.0).astype(BF16)
    for pr in range(npair):
        for half, (kref, vref) in enumerate(((kp_ref, vp_ref), (kc_ref, vc_ref))):
            rows = slice(half * tq, (half + 1) * tq)
            k2 = kref[0, pr]
            v2 = vref[0, pr]
            for hh in range(LANES // HEAD_DIM):
                kh = k2 * keep[hh]
                if half == 0:
                    kh = kh + kb_prev * spare[hh]
                kwin[2 * pr + hh, rows, :] = kh
                vwin[2 * pr + hh, rows, :] = v2 * keep[hh]

    nh = npair * (LANES // HEAD_DIM)
    edge = KV_SPAN - Q_BLOCK

    def body(qb, carry):
        off = pl.multiple_of(qb * Q_BLOCK, Q_BLOCK)
        zs = []
        for h in range(nh):
            pr, hh = divmod(h, LANES // HEAD_DIM)
            q2 = q_ref[0, pr, pl.ds(off, Q_BLOCK), :]
            qm = q2 * keep[hh] + spare[hh]
            z = lax.dot_general(qm, kwin[h, pl.ds(off, KV_SPAN), :], (((1,), (1,)), ((), ())),
                                preferred_element_type=F32)
            zs.append(jnp.concatenate(
                [z[:, :Q_BLOCK] + blo_ref[...], z[:, Q_BLOCK:edge], z[:, edge:] + bhi_ref[...]], axis=1))
        args = []
        for h in range(nh):
            z = zs[h]
            n = jnp.maximum(z, 0.0) + jnp.log(1.0 + jnp.exp2(_neg_abs(z))) * LOG2E
            nb = n.astype(BF16)
            tot = None
            parts = []
            for c0, cw in KEY_CHUNKS:
                part = jnp.dot(nb[:, c0:c0 + cw], tri_ref[0:cw, 0:cw], preferred_element_type=F32)
                parts.append(part if tot is None else part + tot)
                if c0 != 0:
                    rs = jnp.sum(n[:, c0:c0 + cw], axis=-1, keepdims=True)
                    tot = rs if tot is None else tot + rs
            later = jnp.concatenate(parts[::-1], axis=1)
            args.append((z - n) - later)
        for pr in range(npair):
            acc = None
            for hh in range(LANES // HEAD_DIM):
                h = pr * (LANES // HEAD_DIM) + hh
                w = jnp.exp2(args[h]).astype(BF16)
                pv = jnp.dot(w, vwin[h, pl.ds(off, KV_SPAN), :], preferred_element_type=F32)
                acc = pv if acc is None else acc + pv
            a_scr[pr, pl.ds(off, Q_BLOCK), :] = acc
        return carry

    lax.fori_loop(0, nqb, body, 0)

    a = jnp.concatenate([a_scr[j] for j in range(npair)], axis=1)
    an = _rms(a, mna_ref[...]).astype(BF16)
    merged = jnp.concatenate([an, pg_ref[0]], axis=1)
    y = jnp.dot(merged, wout_ref[...], preferred_element_type=F32)
    o_ref[0] = x_ref[0] + mod_ref[0, 2:3, :] * y


def _attn(q, k, v, pg, x, mod, lane_consts, blo, bhi, tri, mna, wout_b, *, tq):
    b, s, d = x.shape
    npair = q.shape[1]
    mixw = wout_b.shape[0]
    const = lambda *shape: pl.BlockSpec(shape, lambda bi, si: (0,) * len(shape))
    cur = pl.BlockSpec((1, npair, tq, LANES), lambda bi, si: (bi, 0, si, 0))
    prev = pl.BlockSpec((1, npair, tq, LANES), lambda bi, si: (bi, 0, jnp.maximum(si - 1, 0), 0))
    return pl.pallas_call(
        functools.partial(_attn_kernel, tq=tq, npair=npair),
        grid=(b, s // tq),
        in_specs=[
            cur, prev, cur, prev, cur,
            pl.BlockSpec((1, tq, pg.shape[2]), lambda bi, si: (bi, si, 0)),
            pl.BlockSpec((1, tq, d), lambda bi, si: (bi, si, 0)),
            pl.BlockSpec((1, N_MOD, d), lambda bi, si: (bi, 0, 0)),
            const(*lane_consts.shape), const(Q_BLOCK, Q_BLOCK), const(Q_BLOCK, Q_BLOCK), const(*tri.shape),
            const(1, npair * LANES), const(mixw, d),
        ],
        out_specs=pl.BlockSpec((1, tq, d), lambda bi, si: (bi, si, 0)),
        out_shape=jax.ShapeDtypeStruct((b, s, d), F32),
        scratch_shapes=[pltpu.VMEM((2 * npair, 2 * tq, LANES), BF16),
                        pltpu.VMEM((2 * npair, 2 * tq, LANES), BF16),
                        pltpu.VMEM((npair, tq, LANES), F32)],
        compiler_params=pltpu.CompilerParams(
            dimension_semantics=("arbitrary", "arbitrary"), vmem_limit_bytes=VMEM_LIMIT),
        name="attn",
    )(q, k, k, v, v, pg, x, mod, lane_consts, blo, bhi, tri, mna, wout_b)


def _ffn_kernel(x_ref, mod_ref, nrm_ref, wg_ref, wu_ref, wd_ref, fn_ref, o_ref, *, final):
    x = x_ref[0]
    h = _rms(x, nrm_ref[...]) * (1.0 + mod_ref[0, 4:5, :]) + mod_ref[0, 3:4, :]
    hb = h.astype(BF16)
    gate = jnp.dot(hb, wg_ref[...], preferred_element_type=F32)
    up = jnp.dot(hb, wu_ref[...], preferred_element_type=F32)
    act = (gate / (1.0 + jnp.exp(-gate)) * up).astype(BF16)
    y = jnp.dot(act, wd_ref[...], preferred_element_type=F32)
    out = x + mod_ref[0, 5:6, :] * y
    if final:
        out = _rms(out, fn_ref[...])
    o_ref[0] = out


def _ffn(x, mod, nrm, wg_b, wu_b, wd_b, fn, *, tm, final):
    b, s, d = x.shape
    dff = wg_b.shape[1]
    once = lambda *shape: pl.BlockSpec(shape, lambda bi, si: (0,) * len(shape),
                                       pipeline_mode=pl.Buffered(1))
    return pl.pallas_call(
        functools.partial(_ffn_kernel, final=final),
        grid=(b, s // tm),
        in_specs=[
            pl.BlockSpec((1, tm, d), lambda bi, si: (bi, si, 0)),
            pl.BlockSpec((1, N_MOD, d), lambda bi, si: (bi, 0, 0)),
            once(1, d), once(d, dff), once(d, dff), once(dff, d), once(1, d),
        ],
        out_specs=pl.BlockSpec((1, tm, d), lambda bi, si: (bi, si, 0)),
        out_shape=jax.ShapeDtypeStruct((b, s, d), F32),
        compiler_params=pltpu.CompilerParams(
            dimension_semantics=("arbitrary", "arbitrary"), vmem_limit_bytes=VMEM_LIMIT),
        name="ffn",
    )(x, mod, nrm, wg_b, wu_b, wd_b, fn)


def _lane_constants():
    lane = np.arange(LANES)
    rows = np.zeros((16, LANES), np.float32)
    rows[0] = lane < HEAD_DIM
    rows[1] = lane >= HEAD_DIM
    rows[2] = lane == HEAD_DIM
    rows[3] = lane == 0
    return rows


def _band_masks():
    a = np.arange(Q_BLOCK)[:, None]
    j = np.arange(Q_BLOCK)[None, :]
    oldest = np.where(j >= a, 0.0, NEG_BIG).astype(np.float32)
    newest = np.where(j < a, 0.0, NEG_BIG).astype(np.float32)
    return oldest, newest


def _suffix_sum_matrix(n):
    j = np.arange(n)
    return (j[:, None] > j[None, :]).astype(np.float32)


def kernel(x, c, w_ada, b_ada, norm_mix_in, w_in, w_pool, pool_scale, sg_norm, w_s, b_s, mix_norm, w_out,
           norm_ffn_in, w_gate_up, w_down, final_norm):
    b, s, d = x.shape
    depth = w_ada.shape[0]
    pw = pool_scale.shape[1]
    sbw = (w_in.shape[2] - 3 * pw) // 3
    dff = w_down.shape[1]
    tm = min(ROW_TILE, s)
    assert s % tm == 0 and tm % Q_BLOCK == 0 and tm >= SB_WINDOW and sbw % LANES == 0

    mod_all = _ada(c, w_ada, b_ada).reshape(depth, b, N_MOD, d)

    q_scale = jnp.concatenate([jnp.full((sbw,), LOG2E * HEAD_DIM ** -0.5, F32),
                               jnp.ones((w_in.shape[2] - sbw,), F32)])
    w_in_b = (w_in * q_scale).astype(BF16)
    groups = w_pool.shape[1]
    gdim = w_pool.shape[2]
    eye = jnp.eye(groups, dtype=F32)
    wpool_bd = (w_pool[:, :, :, None, :] * eye[None, :, None, :, None]).reshape(
        depth, groups * gdim, groups * gdim).astype(BF16)
    causal = jnp.tril(jnp.ones((CHUNK, CHUNK), dtype=bool))
    ws_c = jnp.where(causal[None, None], w_s, 0.0)
    wcat = ws_c.transpose(0, 2, 1, 3).reshape(depth, CHUNK, SG_HEADS * CHUNK).astype(BF16)
    sbias = jnp.repeat(b_s.transpose(0, 2, 1), pw // SG_HEADS, axis=2)
    w_out_b = w_out.astype(BF16)
    wg_b = w_gate_up[:, :, :dff].astype(BF16)
    wu_b = w_gate_up[:, :, dff:].astype(BF16)
    wd_b = w_down.astype(BF16)
    blo, bhi = (jnp.asarray(m) for m in _band_masks())
    lane_consts = jnp.asarray(_lane_constants(), dtype=BF16)
    tri = jnp.asarray(_suffix_sum_matrix(max(w for _, w in KEY_CHUNKS)), dtype=BF16)
    row = lambda v: v.reshape(1, -1)

    for l in range(depth):
        mod = mod_all[l]
        q, k, v, pg = _front(x, mod, row(norm_mix_in[l]), w_in_b[l], wpool_bd[l], row(pool_scale[l]),
                             row(sg_norm[l]), wcat[l], sbias[l], row(mix_norm[l, sbw:sbw + pw]),
                             row(mix_norm[l, sbw + pw:]), tm=tm, sbw=sbw, pw=pw)
        x = _attn(q, k, v, pg, x, mod, lane_consts, blo, bhi, tri, row(mix_norm[l, :sbw]), w_out_b[l], tq=tm)
        x = _ffn(x, mod, row(norm_ffn_in[l]), wg_b[l], wu_b[l], wd_b[l], row(final_norm),
                 tm=tm, final=(l == depth - 1))
    return x
```

```python
import functools

import numpy as np
import jax
import jax.numpy as jnp
from jax import lax
from jax.experimental import pallas as pl
from jax.experimental.pallas import tpu as pltpu

F32 = jnp.float32
BF16 = jnp.bfloat16

HEAD_DIM = 64
Q_BLOCK = 128
SB_WINDOW = 512
KV_SPAN = SB_WINDOW + Q_BLOCK
POOL_WINDOWS = (2, 4, 8, 16)
POOL_HALO = 16
CHUNK = 128
SG_HEADS = 4
N_MOD = 6
EPS = 1e-6
LANES = 128
NEG_BIG = -1e30
LOG2E = 1.4426950408889634

ROW_TILE = 512
VMEM_LIMIT = 56 * 1024 * 1024


def _rms(x, gain):
    return x * lax.rsqrt(jnp.mean(x * x, axis=-1, keepdims=True) + EPS) * gain


def _ada_kernel(c_ref, w_ref, b_ref, o_ref):
    c = c_ref[...]
    c_act = c / (1.0 + jnp.exp(-c))
    o_ref[0] = jnp.dot(c_act, w_ref[0], preferred_element_type=F32,
                       precision=lax.Precision.HIGHEST) + b_ref[0]


def _ada(c, w_ada, b_ada):
    depth, d, nd = w_ada.shape
    b = c.shape[0]
    return pl.pallas_call(
        _ada_kernel,
        grid=(depth, nd // d),
        in_specs=[
            pl.BlockSpec((b, d), lambda l, j: (0, 0)),
            pl.BlockSpec((1, d, d), lambda l, j: (l, 0, j)),
            pl.BlockSpec((1, 1, d), lambda l, j: (l, 0, j)),
        ],
        out_specs=pl.BlockSpec((1, b, d), lambda l, j: (l, 0, j)),
        out_shape=jax.ShapeDtypeStruct((depth, b, nd), F32),
        name="ada_mod",
    )(c, w_ada, b_ada.reshape(depth, 1, nd))


def _front_kernel(x_ref, mod_ref, nrm_ref, w_in_ref, wpool_ref, pscale_ref, sgn_ref, wcat_ref,
                  sbias_ref, mnp_ref, mng_ref, q_ref, k_ref, v_ref, pg_ref, e_ref, *, tm, sbw, pw):
    s = pl.program_id(1)
    x = x_ref[0]
    h = _rms(x, nrm_ref[...]) * (1.0 + mod_ref[0, 1:2, :]) + mod_ref[0, 0:1, :]
    hb = h.astype(BF16)

    npair = sbw // LANES
    for t, ref in enumerate((q_ref, k_ref, v_ref)):
        y = jnp.dot(hb, w_in_ref[:, t * sbw:(t + 1) * sbw], preferred_element_type=F32)
        for j in range(npair):
            ref[0, j] = y[:, j * LANES:(j + 1) * LANES].astype(BF16)

    c0 = 3 * sbw
    p = jnp.dot(hb, w_in_ref[:, c0:c0 + pw], preferred_element_type=F32)

    @pl.when(s == 0)
    def _():
        e_ref[0:POOL_HALO, :] = jnp.zeros((POOL_HALO, pw), F32)

    @pl.when(s > 0)
    def _():
        e_ref[0:POOL_HALO, :] = e_ref[tm:tm + POOL_HALO, :]

    e_ref[POOL_HALO:, :] = p
    e = e_ref[...]
    s2 = e + pltpu.roll(e, 1, 0)
    s4 = s2 + pltpu.roll(s2, 2, 0)
    s8 = s4 + pltpu.roll(s4, 4, 0)
    s16 = s8 + pltpu.roll(s8, 8, 0)
    gdim = pw // len(POOL_WINDOWS)
    lane_e = lax.broadcasted_iota(jnp.int32, (tm + POOL_HALO, pw), 1)
    wsum = jnp.where(lane_e < gdim, s2,
                     jnp.where(lane_e < 2 * gdim, s4, jnp.where(lane_e < 3 * gdim, s8, s16)))
    wsum = wsum[POOL_HALO:, :]
    lane = lax.broadcasted_iota(jnp.int32, (tm, pw), 1)
    pos = s * tm + lax.broadcasted_iota(jnp.int32, (tm, pw), 0)
    wlen = jnp.where(lane < gdim, POOL_WINDOWS[0],
                     jnp.where(lane < 2 * gdim, POOL_WINDOWS[1],
                               jnp.where(lane < 3 * gdim, POOL_WINDOWS[2], POOL_WINDOWS[3])))
    count = jnp.minimum(pos + 1, wlen).astype(F32)
    pooled = wsum / count - p
    mixed = jnp.dot(pooled.astype(BF16), wpool_ref[...], preferred_element_type=F32) * pscale_ref[...]
    pg_ref[0, :, 0:pw] = _rms(mixed, mnp_ref[...]).astype(BF16)

    c1 = c0 + pw
    u = jnp.dot(hb, w_in_ref[:, c1:c1 + pw], preferred_element_type=F32)
    vg = jnp.dot(hb, w_in_ref[:, c1 + pw:c1 + 2 * pw], preferred_element_type=F32)
    ug = jax.nn.gelu(u)
    vge = jax.nn.gelu(vg)
    mu = jnp.mean(vge, axis=-1, keepdims=True)
    var = jnp.mean(jnp.square(vge - mu), axis=-1, keepdims=True)
    vn = ((vge - mu) * lax.rsqrt(var + EPS) * sgn_ref[...]).astype(BF16)
    hd = pw // SG_HEADS
    lane_c = lax.broadcasted_iota(jnp.int32, (CHUNK, pw), 1)
    zero = jnp.zeros((CHUNK, pw), BF16)
    outs = []
    for c in range(tm // CHUNK):
        vc = vn[c * CHUNK:(c + 1) * CHUNK, :]
        stacked = jnp.concatenate(
            [jnp.where((lane_c >= hh * hd) & (lane_c < (hh + 1) * hd), vc, zero) for hh in range(SG_HEADS)],
            axis=0)
        outs.append(jnp.dot(wcat_ref[...], stacked, preferred_element_type=F32) + sbias_ref[...])
    g_out = ug * jnp.concatenate(outs, axis=0)
    pg_ref[0, :, pw:2 * pw] = _rms(g_out, mng_ref[...]).astype(BF16)


def _front(x, mod, nrm, w_in_b, wpool_bd, pscale, sgn, wcat, sbias, mnp, mng, *, tm, sbw, pw):
    b, s, d = x.shape
    npair = sbw // LANES
    in_w = w_in_b.shape[1]
    const = lambda *shape: pl.BlockSpec(shape, lambda bi, si: (0,) * len(shape))
    qkv_spec = pl.BlockSpec((1, npair, tm, LANES), lambda bi, si: (bi, 0, si, 0))
    qkv_shape = jax.ShapeDtypeStruct((b, npair, s, LANES), BF16)
    return pl.pallas_call(
        functools.partial(_front_kernel, tm=tm, sbw=sbw, pw=pw),
        grid=(b, s // tm),
        in_specs=[
            pl.BlockSpec((1, tm, d), lambda bi, si: (bi, si, 0)),
            pl.BlockSpec((1, N_MOD, d), lambda bi, si: (bi, 0, 0)),
            const(1, d), const(d, in_w), const(pw, pw), const(1, pw), const(1, pw),
            const(CHUNK, SG_HEADS * CHUNK), const(CHUNK, pw), const(1, pw), const(1, pw),
        ],
        out_specs=[qkv_spec, qkv_spec, qkv_spec,
                   pl.BlockSpec((1, tm, 2 * pw), lambda bi, si: (bi, si, 0))],
        out_shape=[qkv_shape, qkv_shape, qkv_shape, jax.ShapeDtypeStruct((b, s, 2 * pw), BF16)],
        scratch_shapes=[pltpu.VMEM((tm + POOL_HALO, pw), F32)],
        compiler_params=pltpu.CompilerParams(
            dimension_semantics=("arbitrary", "arbitrary"), vmem_limit_bytes=VMEM_LIMIT),
        name="front",
    )(x, mod, nrm, w_in_b, wpool_bd, pscale, sgn, wcat, sbias, mnp, mng)


KEY_CHUNKS = ((512, 128), (256, 256), (0, 256))


def _neg_abs(z):
    sign = jnp.uint32(0x80000000)
    return lax.bitcast_convert_type(lax.bitcast_convert_type(z, jnp.uint32) | sign, F32)


def _attn_kernel(q_ref, kp_ref, kc_ref, vp_ref, vc_ref, pg_ref, x_ref, mod_ref, lc_ref, blo_ref, bhi_ref,
                 tri_ref, mna_ref, wout_ref, o_ref, kwin, vwin, a_scr, *, tq, npair):
    i = pl.program_id(1)
    nqb = tq // Q_BLOCK
    keep = (lc_ref[0:1, :], lc_ref[1:2, :])
    spare = (lc_ref[2:3, :], lc_ref[3:4, :])
    kb_prev = jnp.where(i == 0, NEG_BIG, 0.0).astype(BF16)
    for pr in range(npair):
        for half, (kref, vref) in enumerate(((kp_ref, vp_ref), (kc_ref, vc_ref))):
            rows = slice(half * tq, (half + 1) * tq)
            k2 = kref[0, pr]
            v2 = vref[0, pr]
            for hh in range(LANES // HEAD_DIM):
                kh = k2 * keep[hh]
                if half == 0:
                    kh = kh + kb_prev * spare[hh]
                kwin[2 * pr + hh, rows, :] = kh
                vwin[2 * pr + hh, rows, :] = v2 * keep[hh]

    nh = npair * (LANES // HEAD_DIM)
    edge = KV_SPAN - Q_BLOCK

    def scores(qb):
        off = qb * Q_BLOCK
        zs = []
        for h in range(nh):
            pr, hh = divmod(h, LANES // HEAD_DIM)
            q2 = q_ref[0, pr, off:off + Q_BLOCK, :]
            qm = q2 * keep[hh] + spare[hh]
            z = lax.dot_general(qm, kwin[h, off:off + KV_SPAN, :], (((1,), (1,)), ((), ())),
                                preferred_element_type=F32)
            zs.append(jnp.concatenate(
                [z[:, :Q_BLOCK] + blo_ref[...], z[:, Q_BLOCK:edge], z[:, edge:] + bhi_ref[...]], axis=1))
        return zs

    def log_weights(zs):
        args = []
        for z in zs:
            n = jnp.maximum(z, 0.0) + jnp.log(1.0 + jnp.exp2(_neg_abs(z))) * LOG2E
            nb = n.astype(BF16)
            tot = None
            parts = []
            for c0, cw in KEY_CHUNKS:
                part = jnp.dot(nb[:, c0:c0 + cw], tri_ref[0:cw, 0:cw], preferred_element_type=F32)
                parts.append(part if tot is None else part + tot)
                if c0 != 0:
                    rs = jnp.sum(n[:, c0:c0 + cw], axis=-1, keepdims=True)
                    tot = rs if tot is None else tot + rs
            later = jnp.concatenate(parts[::-1], axis=1)
            args.append((z - n) - later)
        return args

    def weighted_values(qb, args):
        off = qb * Q_BLOCK
        for pr in range(npair):
            acc = None
            for hh in range(LANES // HEAD_DIM):
                h = pr * (LANES // HEAD_DIM) + hh
                w = jnp.exp2(args[h]).astype(BF16)
                pv = jnp.dot(w, vwin[h, off:off + KV_SPAN, :], preferred_element_type=F32)
                acc = pv if acc is None else acc + pv
            a_scr[pr, off:off + Q_BLOCK, :] = acc

    zs = scores(0)
    for qb in range(nqb):
        args = log_weights(zs)
        if qb + 1 < nqb:
            zs = scores(qb + 1)
        weighted_values(qb, args)

    a = jnp.concatenate([a_scr[j] for j in range(npair)], axis=1)
    an = _rms(a, mna_ref[...]).astype(BF16)
    merged = jnp.concatenate([an, pg_ref[0]], axis=1)
    y = jnp.dot(merged, wout_ref[...], preferred_element_type=F32)
    o_ref[0] = x_ref[0] + mod_ref[0, 2:3, :] * y


def _attn(q, k, v, pg, x, mod, lane_consts, blo, bhi, tri, mna, wout_b, *, tq):
    b, s, d = x.shape
    npair = q.shape[1]
    mixw = wout_b.shape[0]
    const = lambda *shape: pl.BlockSpec(shape, lambda bi, si: (0,) * len(shape))
    cur = pl.BlockSpec((1, npair, tq, LANES), lambda bi, si: (bi, 0, si, 0))
    prev = pl.BlockSpec((1, npair, tq, LANES), lambda bi, si: (bi, 0, jnp.maximum(si - 1, 0), 0))
    return pl.pallas_call(
        functools.partial(_attn_kernel, tq=tq, npair=npair),
        grid=(b, s // tq),
        in_specs=[
            cur, prev, cur, prev, cur,
            pl.BlockSpec((1, tq, pg.shape[2]), lambda bi, si: (bi, si, 0)),
            pl.BlockSpec((1, tq, d), lambda bi, si: (bi, si, 0)),
            pl.BlockSpec((1, N_MOD, d), lambda bi, si: (bi, 0, 0)),
            const(*lane_consts.shape), const(Q_BLOCK, Q_BLOCK), const(Q_BLOCK, Q_BLOCK), const(*tri.shape),
            const(1, npair * LANES), const(mixw, d),
        ],
        out_specs=pl.BlockSpec((1, tq, d), lambda bi, si: (bi, si, 0)),
        out_shape=jax.ShapeDtypeStruct((b, s, d), F32),
        scratch_shapes=[pltpu.VMEM((2 * npair, 2 * tq, LANES), BF16),
                        pltpu.VMEM((2 * npair, 2 * tq, LANES), BF16),
                        pltpu.VMEM((npair, tq, LANES), F32)],
        compiler_params=pltpu.CompilerParams(
            dimension_semantics=("arbitrary", "arbitrary"), vmem_limit_bytes=VMEM_LIMIT),
        name="attn",
    )(q, k, k, v, v, pg, x, mod, lane_consts, blo, bhi, tri, mna, wout_b)


def _ffn_kernel(x_ref, mod_ref, nrm_ref, wg_ref, wu_ref, wd_ref, fn_ref, o_ref, *, final):
    x = x_ref[0]
    h = _rms(x, nrm_ref[...]) * (1.0 + mod_ref[0, 4:5, :]) + mod_ref[0, 3:4, :]
    hb = h.astype(BF16)
    gate = jnp.dot(hb, wg_ref[...], preferred_element_type=F32)
    up = jnp.dot(hb, wu_ref[...], preferred_element_type=F32)
    act = (gate / (1.0 + jnp.exp(-gate)) * up).astype(BF16)
    y = jnp.dot(act, wd_ref[...], preferred_element_type=F32)
    out = x + mod_ref[0, 5:6, :] * y
    if final:
        out = _rms(out, fn_ref[...])
    o_ref[0] = out


def _ffn(x, mod, nrm, wg_b, wu_b, wd_b, fn, *, tm, final):
    b, s, d = x.shape
    dff = wg_b.shape[1]
    once = lambda *shape: pl.BlockSpec(shape, lambda bi, si: (0,) * len(shape),
                                       pipeline_mode=pl.Buffered(1))
    return pl.pallas_call(
        functools.partial(_ffn_kernel, final=final),
        grid=(b, s // tm),
        in_specs=[
            pl.BlockSpec((1, tm, d), lambda bi, si: (bi, si, 0)),
            pl.BlockSpec((1, N_MOD, d), lambda bi, si: (bi, 0, 0)),
            once(1, d), once(d, dff), once(d, dff), once(dff, d), once(1, d),
        ],
        out_specs=pl.BlockSpec((1, tm, d), lambda bi, si: (bi, si, 0)),
        out_shape=jax.ShapeDtypeStruct((b, s, d), F32),
        compiler_params=pltpu.CompilerParams(
            dimension_semantics=("arbitrary", "arbitrary"), vmem_limit_bytes=VMEM_LIMIT),
        name="ffn",
    )(x, mod, nrm, wg_b, wu_b, wd_b, fn)


def _lane_constants():
    lane = np.arange(LANES)
    rows = np.zeros((16, LANES), np.float32)
    rows[0] = lane < HEAD_DIM
    rows[1] = lane >= HEAD_DIM
    rows[2] = lane == HEAD_DIM
    rows[3] = lane == 0
    return rows


def _band_masks():
    a = np.arange(Q_BLOCK)[:, None]
    j = np.arange(Q_BLOCK)[None, :]
    oldest = np.where(j >= a, 0.0, NEG_BIG).astype(np.float32)
    newest = np.where(j < a, 0.0, NEG_BIG).astype(np.float32)
    return oldest, newest


def _suffix_sum_matrix(n):
    j = np.arange(n)
    return (j[:, None] > j[None, :]).astype(np.float32)


def kernel(x, c, w_ada, b_ada, norm_mix_in, w_in, w_pool, pool_scale, sg_norm, w_s, b_s, mix_norm, w_out,
           norm_ffn_in, w_gate_up, w_down, final_norm):
    b, s, d = x.shape
    depth = w_ada.shape[0]
    pw = pool_scale.shape[1]
    sbw = (w_in.shape[2] - 3 * pw) // 3
    dff = w_down.shape[1]
    tm = min(ROW_TILE, s)
    assert s % tm == 0 and tm % Q_BLOCK == 0 and tm >= SB_WINDOW and sbw % LANES == 0

    mod_all = _ada(c, w_ada, b_ada).reshape(depth, b, N_MOD, d)

    q_scale = jnp.concatenate([jnp.full((sbw,), LOG2E * HEAD_DIM ** -0.5, F32),
                               jnp.ones((w_in.shape[2] - sbw,), F32)])
    w_in_b = (w_in * q_scale).astype(BF16)
    groups = w_pool.shape[1]
    gdim = w_pool.shape[2]
    eye = jnp.eye(groups, dtype=F32)
    wpool_bd = (w_pool[:, :, :, None, :] * eye[None, :, None, :, None]).reshape(
        depth, groups * gdim, groups * gdim).astype(BF16)
    causal = jnp.tril(jnp.ones((CHUNK, CHUNK), dtype=bool))
    ws_c = jnp.where(causal[None, None], w_s, 0.0)
    wcat = ws_c.transpose(0, 2, 1, 3).reshape(depth, CHUNK, SG_HEADS * CHUNK).astype(BF16)
    sbias = jnp.repeat(b_s.transpose(0, 2, 1), pw // SG_HEADS, axis=2)
    w_out_b = w_out.astype(BF16)
    wg_b = w_gate_up[:, :, :dff].astype(BF16)
    wu_b = w_gate_up[:, :, dff:].astype(BF16)
    wd_b = w_down.astype(BF16)
    blo, bhi = (jnp.asarray(m) for m in _band_masks())
    lane_consts = jnp.asarray(_lane_constants(), dtype=BF16)
    tri = jnp.asarray(_suffix_sum_matrix(max(w for _, w in KEY_CHUNKS)), dtype=BF16)
    row = lambda v: v.reshape(1, -1)

    for l in range(depth):
        mod = mod_all[l]
        q, k, v, pg = _front(x, mod, row(norm_mix_in[l]), w_in_b[l], wpool_bd[l], row(pool_scale[l]),
                             row(sg_norm[l]), wcat[l], sbias[l], row(mix_norm[l, sbw:sbw + pw]),
                             row(mix_norm[l, sbw + pw:]), tm=tm, sbw=sbw, pw=pw)
        x = _attn(q, k, v, pg, x, mod, lane_consts, blo, bhi, tri, row(mix_norm[l, :sbw]), w_out_b[l], tq=tm)
        x = _ffn(x, mod, row(norm_ffn_in[l]), wg_b[l], wu_b[l], wd_b[l], row(final_norm),
                 tm=tm, final=(l == depth - 1))
    return x
```

```python
import functools

import numpy as np
import jax
import jax.numpy as jnp
from jax import lax
from jax.experimental import pallas as pl
from jax.experimental.pallas import tpu as pltpu

F32 = jnp.float32
BF16 = jnp.bfloat16

HEAD_DIM = 64
Q_BLOCK = 128
SB_WINDOW = 512
KV_SPAN = SB_WINDOW + Q_BLOCK
POOL_WINDOWS = (2, 4, 8, 16)
POOL_HALO = 16
CHUNK = 128
SG_HEADS = 4
N_MOD = 6
EPS = 1e-6
LANES = 128
NEG_BIG = -1e30
LOG2E = 1.4426950408889634

ROW_TILE = 512
VMEM_LIMIT = 56 * 1024 * 1024


def _spare_lane(head_in_group):
    return HEAD_DIM if head_in_group == 0 else 0


def _rms(x, gain):
    return x * lax.rsqrt(jnp.mean(x * x, axis=-1, keepdims=True) + EPS) * gain


def _ada_kernel(c_ref, w_ref, b_ref, o_ref):
    c = c_ref[...]
    c_act = c / (1.0 + jnp.exp(-c))
    o_ref[0] = jnp.dot(c_act, w_ref[0], preferred_element_type=F32,
                       precision=lax.Precision.HIGHEST) + b_ref[0]


def _ada(c, w_ada, b_ada):
    depth, d, nd = w_ada.shape
    b = c.shape[0]
    return pl.pallas_call(
        _ada_kernel,
        grid=(depth, nd // d),
        in_specs=[
            pl.BlockSpec((b, d), lambda l, j: (0, 0)),
            pl.BlockSpec((1, d, d), lambda l, j: (l, 0, j)),
            pl.BlockSpec((1, 1, d), lambda l, j: (l, 0, j)),
        ],
        out_specs=pl.BlockSpec((1, b, d), lambda l, j: (l, 0, j)),
        out_shape=jax.ShapeDtypeStruct((depth, b, nd), F32),
        name="ada_mod",
    )(c, w_ada, b_ada.reshape(depth, 1, nd))


def _front_kernel(x_ref, mod_ref, nrm_ref, w_in_ref, wpool_ref, pscale_ref, sgn_ref, wcat_ref,
                  sbias_ref, mnp_ref, mng_ref, q_ref, k_ref, v_ref, pg_ref, e_ref, *, tm, sbw, pw):
    s = pl.program_id(1)
    x = x_ref[0]
    h = _rms(x, nrm_ref[...]) * (1.0 + mod_ref[0, 1:2, :]) + mod_ref[0, 0:1, :]
    hb = h.astype(BF16)

    lane_g = lax.broadcasted_iota(jnp.int32, (tm, LANES), 1)
    heads_per_group = LANES // HEAD_DIM
    for t, ref in enumerate((q_ref, k_ref, v_ref)):
        y = jnp.dot(hb, w_in_ref[:, t * sbw:(t + 1) * sbw], preferred_element_type=F32)
        for j in range(sbw // LANES):
            grp = y[:, j * LANES:(j + 1) * LANES]
            for hh in range(heads_per_group):
                own = (lane_g >= hh * HEAD_DIM) & (lane_g < (hh + 1) * HEAD_DIM)
                fill = jnp.where(lane_g == _spare_lane(hh), 1.0, 0.0) if t == 0 else 0.0
                ref[0, heads_per_group * j + hh] = jnp.where(own, grp, fill).astype(BF16)

    c0 = 3 * sbw
    p = jnp.dot(hb, w_in_ref[:, c0:c0 + pw], preferred_element_type=F32)

    @pl.when(s == 0)
    def _():
        e_ref[0:POOL_HALO, :] = jnp.zeros((POOL_HALO, pw), F32)

    @pl.when(s > 0)
    def _():
        e_ref[0:POOL_HALO, :] = e_ref[tm:tm + POOL_HALO, :]

    e_ref[POOL_HALO:, :] = p
    e = e_ref[...]
    s2 = e + pltpu.roll(e, 1, 0)
    s4 = s2 + pltpu.roll(s2, 2, 0)
    s8 = s4 + pltpu.roll(s4, 4, 0)
    s16 = s8 + pltpu.roll(s8, 8, 0)
    gdim = pw // len(POOL_WINDOWS)
    lane_e = lax.broadcasted_iota(jnp.int32, (tm + POOL_HALO, pw), 1)
    wsum = jnp.where(lane_e < gdim, s2,
                     jnp.where(lane_e < 2 * gdim, s4, jnp.where(lane_e < 3 * gdim, s8, s16)))
    wsum = wsum[POOL_HALO:, :]
    lane = lax.broadcasted_iota(jnp.int32, (tm, pw), 1)
    pos = s * tm + lax.broadcasted_iota(jnp.int32, (tm, pw), 0)
    wlen = jnp.where(lane < gdim, POOL_WINDOWS[0],
                     jnp.where(lane < 2 * gdim, POOL_WINDOWS[1],
                               jnp.where(lane < 3 * gdim, POOL_WINDOWS[2], POOL_WINDOWS[3])))
    count = jnp.minimum(pos + 1, wlen).astype(F32)
    pooled = wsum / count - p
    mixed = jnp.dot(pooled.astype(BF16), wpool_ref[...], preferred_element_type=F32) * pscale_ref[...]
    pg_ref[0, :, 0:pw] = _rms(mixed, mnp_ref[...]).astype(BF16)

    c1 = c0 + pw
    u = jnp.dot(hb, w_in_ref[:, c1:c1 + pw], preferred_element_type=F32)
    vg = jnp.dot(hb, w_in_ref[:, c1 + pw:c1 + 2 * pw], preferred_element_type=F32)
    ug = jax.nn.gelu(u)
    vge = jax.nn.gelu(vg)
    mu = jnp.mean(vge, axis=-1, keepdims=True)
    var = jnp.mean(jnp.square(vge - mu), axis=-1, keepdims=True)
    vn = ((vge - mu) * lax.rsqrt(var + EPS) * sgn_ref[...]).astype(BF16)
    hd = pw // SG_HEADS
    lane_c = lax.broadcasted_iota(jnp.int32, (CHUNK, pw), 1)
    zero = jnp.zeros((CHUNK, pw), BF16)
    outs = []
    for c in range(tm // CHUNK):
        vc = vn[c * CHUNK:(c + 1) * CHUNK, :]
        stacked = jnp.concatenate(
            [jnp.where((lane_c >= hh * hd) & (lane_c < (hh + 1) * hd), vc, zero) for hh in range(SG_HEADS)],
            axis=0)
        outs.append(jnp.dot(wcat_ref[...], stacked, preferred_element_type=F32) + sbias_ref[...])
    g_out = ug * jnp.concatenate(outs, axis=0)
    pg_ref[0, :, pw:2 * pw] = _rms(g_out, mng_ref[...]).astype(BF16)


def _front(x, mod, nrm, w_in_b, wpool_bd, pscale, sgn, wcat, sbias, mnp, mng, *, tm, sbw, pw):
    b, s, d = x.shape
    nh = sbw // HEAD_DIM
    in_w = w_in_b.shape[1]
    const = lambda *shape: pl.BlockSpec(shape, lambda bi, si: (0,) * len(shape))
    qkv_spec = pl.BlockSpec((1, nh, tm, LANES), lambda bi, si: (bi, 0, si, 0))
    qkv_shape = jax.ShapeDtypeStruct((b, nh, s, LANES), BF16)
    return pl.pallas_call(
        functools.partial(_front_kernel, tm=tm, sbw=sbw, pw=pw),
        grid=(b, s // tm),
        in_specs=[
            pl.BlockSpec((1, tm, d), lambda bi, si: (bi, si, 0)),
            pl.BlockSpec((1, N_MOD, d), lambda bi, si: (bi, 0, 0)),
            const(1, d), const(d, in_w), const(pw, pw), const(1, pw), const(1, pw),
            const(CHUNK, SG_HEADS * CHUNK), const(CHUNK, pw), const(1, pw), const(1, pw),
        ],
        out_specs=[qkv_spec, qkv_spec, qkv_spec,
                   pl.BlockSpec((1, tm, 2 * pw), lambda bi, si: (bi, si, 0))],
        out_shape=[qkv_shape, qkv_shape, qkv_shape, jax.ShapeDtypeStruct((b, s, 2 * pw), BF16)],
        scratch_shapes=[pltpu.VMEM((tm + POOL_HALO, pw), F32)],
        compiler_params=pltpu.CompilerParams(
            dimension_semantics=("arbitrary", "arbitrary"), vmem_limit_bytes=VMEM_LIMIT),
        name="front",
    )(x, mod, nrm, w_in_b, wpool_bd, pscale, sgn, wcat, sbias, mnp, mng)


TRI_WIDTH = 256


def _neg_abs(z):
    sign = jnp.uint32(0x80000000)
    return lax.bitcast_convert_type(lax.bitcast_convert_type(z, jnp.uint32) | sign, F32)


def _attn_kernel(q_ref, kp_ref, kc_ref, vp_ref, vc_ref, pg_ref, x_ref, mod_ref, lc_ref, tri_ref, mna_ref,
                 wout_ref, o_ref, *, tq, nh):
    i = pl.program_id(1)
    nqb = tq // Q_BLOCK
    edge = KV_SPAN - Q_BLOCK
    group = LANES // HEAD_DIM
    assert KV_SPAN - 2 * Q_BLOCK + Q_BLOCK == 2 * TRI_WIDTH
    kb_prev = jnp.where(i == 0, NEG_BIG, 0.0).astype(BF16)
    prev_bias = [kb_prev * lc_ref[hh:hh + 1, :] for hh in range(group)]

    def window(prev_ref, cur_ref, h, off):
        return prev_ref[0, h, off:, :], cur_ref[0, h, 0:off + Q_BLOCK, :]

    newest = (lax.broadcasted_iota(jnp.int32, (Q_BLOCK, Q_BLOCK), 1)
              < lax.broadcasted_iota(jnp.int32, (Q_BLOCK, Q_BLOCK), 0))
    half = TRI_WIDTH - Q_BLOCK

    def scores(item):
        qb, h = divmod(item, nh)
        off = qb * Q_BLOCK
        k_prev, k_cur = window(kp_ref, kc_ref, h, off)
        keys = jnp.concatenate([k_prev + prev_bias[h % group], k_cur], axis=0)
        z = lax.dot_general(q_ref[0, h, off:off + Q_BLOCK, :], keys, (((1,), (1,)), ((), ())),
                            preferred_element_type=F32)
        return jnp.concatenate([jnp.where(newest, z[:, edge:], z[:, :Q_BLOCK]), z[:, Q_BLOCK:edge]], axis=1)

    def softplus(z):
        n = jnp.maximum(z, 0.0) + jnp.log(1.0 + jnp.exp2(_neg_abs(z))) * LOG2E
        nb = n.astype(BF16)
        n_new = jnp.where(newest, n[:, :Q_BLOCK], 0.0)
        nb_new = n_new.astype(BF16)
        nb_old = nb[:, :Q_BLOCK] - nb_new
        tot_new = jnp.sum(n_new, axis=-1, keepdims=True)
        tot_mid = jnp.sum(n[:, Q_BLOCK + half:], axis=-1, keepdims=True)
        chunks = (jnp.concatenate([nb_old, nb[:, Q_BLOCK:Q_BLOCK + half]], axis=1), nb[:, Q_BLOCK + half:], nb_new)
        return z - n, chunks, (tot_mid + tot_new, tot_new)

    def log_weight(state):
        dmn, (c_first, c_mid, c_new), (tot_first, tot_mid) = state
        l_first = jnp.dot(c_first, tri_ref[...], preferred_element_type=F32) + tot_first
        l_mid = jnp.dot(c_mid, tri_ref[...], preferred_element_type=F32) + tot_mid
        l_new = jnp.dot(c_new, tri_ref[0:Q_BLOCK, 0:Q_BLOCK], preferred_element_type=F32)
        later = jnp.concatenate([jnp.where(newest, l_new, l_first[:, :Q_BLOCK]), l_first[:, Q_BLOCK:], l_mid],
                                axis=1)
        return dmn - later

    def weights(arg):
        w = jnp.exp2(arg)
        wb = w.astype(BF16)
        w_new = jnp.where(newest, w[:, :Q_BLOCK], 0.0).astype(BF16)
        return jnp.concatenate([wb[:, :Q_BLOCK] - w_new, wb[:, Q_BLOCK:], w_new], axis=1)

    def weighted_values(item, w):
        qb, h = divmod(item, nh)
        values = jnp.concatenate(window(vp_ref, vc_ref, h, qb * Q_BLOCK), axis=0)
        return jnp.dot(w, values, preferred_element_type=F32)

    def project(qb, pvs):
        a = jnp.concatenate([sum(pvs[g * group + 1:(g + 1) * group], pvs[g * group])
                             for g in range(nh // group)], axis=1)
        rows = slice(qb * Q_BLOCK, (qb + 1) * Q_BLOCK)
        an = _rms(a, mna_ref[...]).astype(BF16)
        merged = jnp.concatenate([an, pg_ref[0, rows, :]], axis=1)
        y = jnp.dot(merged, wout_ref[...], preferred_element_type=F32)
        o_ref[0, rows, :] = x_ref[0, rows, :] + mod_ref[0, 2:3, :] * y

    zs = [scores(h) for h in range(nh)]
    for qb in range(nqb):
        args = [log_weight(softplus(z)) for z in zs]
        if qb + 1 < nqb:
            zs = [scores((qb + 1) * nh + h) for h in range(nh)]
        project(qb, [weighted_values(qb * nh + h, weights(args[h])) for h in range(nh)])


def _attn(q, k, v, pg, x, mod, lane_consts, tri, mna, wout_b, *, tq):
    b, s, d = x.shape
    nh = q.shape[1]
    mixw = wout_b.shape[0]
    const = lambda *shape: pl.BlockSpec(shape, lambda bi, si: (0,) * len(shape))
    cur = pl.BlockSpec((1, nh, tq, LANES), lambda bi, si: (bi, 0, si, 0))
    prev = pl.BlockSpec((1, nh, tq, LANES), lambda bi, si: (bi, 0, jnp.maximum(si - 1, 0), 0))
    return pl.pallas_call(
        functools.partial(_attn_kernel, tq=tq, nh=nh),
        grid=(b, s // tq),
        in_specs=[
            cur, prev, cur, prev, cur,
            pl.BlockSpec((1, tq, pg.shape[2]), lambda bi, si: (bi, si, 0)),
            pl.BlockSpec((1, tq, d), lambda bi, si: (bi, si, 0)),
            pl.BlockSpec((1, N_MOD, d), lambda bi, si: (bi, 0, 0)),
            const(*lane_consts.shape), const(*tri.shape),
            const(1, nh * HEAD_DIM), const(mixw, d),
        ],
        out_specs=pl.BlockSpec((1, tq, d), lambda bi, si: (bi, si, 0)),
        out_shape=jax.ShapeDtypeStruct((b, s, d), F32),
        compiler_params=pltpu.CompilerParams(
            dimension_semantics=("arbitrary", "arbitrary"), vmem_limit_bytes=VMEM_LIMIT),
        name="attn",
    )(q, k, k, v, v, pg, x, mod, lane_consts, tri, mna, wout_b)


def _ffn_kernel(x_ref, mod_ref, nrm_ref, wg_ref, wu_ref, wd_ref, fn_ref, o_ref, *, final):
    x = x_ref[0]
    h = _rms(x, nrm_ref[...]) * (1.0 + mod_ref[0, 4:5, :]) + mod_ref[0, 3:4, :]
    hb = h.astype(BF16)
    gate = jnp.dot(hb, wg_ref[...], preferred_element_type=F32)
    up = jnp.dot(hb, wu_ref[...], preferred_element_type=F32)
    act = (gate / (1.0 + jnp.exp(-gate)) * up).astype(BF16)
    y = jnp.dot(act, wd_ref[...], preferred_element_type=F32)
    out = x + mod_ref[0, 5:6, :] * y
    if final:
        out = _rms(out, fn_ref[...])
    o_ref[0] = out


def _ffn(x, mod, nrm, wg_b, wu_b, wd_b, fn, *, tm, final):
    b, s, d = x.shape
    dff = wg_b.shape[1]
    once = lambda *shape: pl.BlockSpec(shape, lambda bi, si: (0,) * len(shape),
                                       pipeline_mode=pl.Buffered(1))
    return pl.pallas_call(
        functools.partial(_ffn_kernel, final=final),
        grid=(b, s // tm),
        in_specs=[
            pl.BlockSpec((1, tm, d), lambda bi, si: (bi, si, 0)),
            pl.BlockSpec((1, N_MOD, d), lambda bi, si: (bi, 0, 0)),
            once(1, d), once(d, dff), once(d, dff), once(dff, d), once(1, d),
        ],
        out_specs=pl.BlockSpec((1, tm, d), lambda bi, si: (bi, si, 0)),
        out_shape=jax.ShapeDtypeStruct((b, s, d), F32),
        compiler_params=pltpu.CompilerParams(
            dimension_semantics=("arbitrary", "arbitrary"), vmem_limit_bytes=VMEM_LIMIT),
        name="ffn",
    )(x, mod, nrm, wg_b, wu_b, wd_b, fn)


def _lane_constants():
    rows = np.zeros((16, LANES), np.float32)
    for hh in range(LANES // HEAD_DIM):
        rows[hh, _spare_lane(hh)] = 1.0
    return rows


def _suffix_sum_matrix(n):
    j = np.arange(n)
    return (j[:, None] > j[None, :]).astype(np.float32)


def kernel(x, c, w_ada, b_ada, norm_mix_in, w_in, w_pool, pool_scale, sg_norm, w_s, b_s, mix_norm, w_out,
           norm_ffn_in, w_gate_up, w_down, final_norm):
    b, s, d = x.shape
    depth = w_ada.shape[0]
    pw = pool_scale.shape[1]
    sbw = (w_in.shape[2] - 3 * pw) // 3
    dff = w_down.shape[1]
    tm = min(ROW_TILE, s)
    assert s % tm == 0 and tm % Q_BLOCK == 0 and tm >= SB_WINDOW and sbw % LANES == 0

    mod_all = _ada(c, w_ada, b_ada).reshape(depth, b, N_MOD, d)

    q_scale = jnp.concatenate([jnp.full((sbw,), LOG2E * HEAD_DIM ** -0.5, F32),
                               jnp.ones((w_in.shape[2] - sbw,), F32)])
    w_in_b = (w_in * q_scale).astype(BF16)
    groups = w_pool.shape[1]
    gdim = w_pool.shape[2]
    eye = jnp.eye(groups, dtype=F32)
    wpool_bd = (w_pool[:, :, :, None, :] * eye[None, :, None, :, None]).reshape(
        depth, groups * gdim, groups * gdim).astype(BF16)
    causal = jnp.tril(jnp.ones((CHUNK, CHUNK), dtype=bool))
    ws_c = jnp.where(causal[None, None], w_s, 0.0)
    wcat = ws_c.transpose(0, 2, 1, 3).reshape(depth, CHUNK, SG_HEADS * CHUNK).astype(BF16)
    sbias = jnp.repeat(b_s.transpose(0, 2, 1), pw // SG_HEADS, axis=2)
    w_out_b = w_out.astype(BF16)
    wg_b = w_gate_up[:, :, :dff].astype(BF16)
    wu_b = w_gate_up[:, :, dff:].astype(BF16)
    wd_b = w_down.astype(BF16)
    lane_consts = jnp.asarray(_lane_constants(), dtype=BF16)
    tri = jnp.asarray(_suffix_sum_matrix(TRI_WIDTH), dtype=BF16)
    row = lambda v: v.reshape(1, -1)

    for l in range(depth):
        mod = mod_all[l]
        q, k, v, pg = _front(x, mod, row(norm_mix_in[l]), w_in_b[l], wpool_bd[l], row(pool_scale[l]),
                             row(sg_norm[l]), wcat[l], sbias[l], row(mix_norm[l, sbw:sbw + pw]),
                             row(mix_norm[l, sbw + pw:]), tm=tm, sbw=sbw, pw=pw)
        x = _attn(q, k, v, pg, x, mod, lane_consts, tri, row(mix_norm[l, :sbw]), w_out_b[l], tq=tm)
        x = _ffn(x, mod, row(norm_ffn_in[l]), wg_b[l], wu_b[l], wd_b[l], row(final_norm),
                 tm=tm, final=(l == depth - 1))
    return x
```

```python
import functools

import numpy as np
import jax
import jax.numpy as jnp
from jax import lax
from jax.experimental import pallas as pl
from jax.experimental.pallas import tpu as pltpu

F32 = jnp.float32
BF16 = jnp.bfloat16

HEAD_DIM = 64
Q_BLOCK = 128
SB_WINDOW = 512
KV_SPAN = SB_WINDOW + Q_BLOCK
POOL_WINDOWS = (2, 4, 8, 16)
POOL_HALO = 16
CHUNK = 128
SG_HEADS = 4
N_MOD = 6
EPS = 1e-6
LANES = 128
NEG_BIG = -1e30
LOG2E = 1.4426950408889634

ROW_TILE = 512
FRONT_TILE = 1024
FFN_TILE = 1024
FRONT_SUB = 256
FFN_SUB = 256
VMEM_LIMIT = 56 * 1024 * 1024


def _spare_lane(head_in_group):
    return HEAD_DIM if head_in_group == 0 else 0


def _rms(x, gain):
    return x * lax.rsqrt(jnp.mean(x * x, axis=-1, keepdims=True) + EPS) * gain


def _ada_kernel(c_ref, w_ref, b_ref, o_ref):
    c = c_ref[...]
    c_act = c / (1.0 + jnp.exp(-c))
    o_ref[0] = jnp.dot(c_act, w_ref[0], preferred_element_type=F32,
                       precision=lax.Precision.HIGHEST) + b_ref[0]


def _ada(c, w_ada, b_ada):
    depth, d, nd = w_ada.shape
    b = c.shape[0]
    return pl.pallas_call(
        _ada_kernel,
        grid=(depth, nd // d),
        in_specs=[
            pl.BlockSpec((b, d), lambda l, j: (0, 0)),
            pl.BlockSpec((1, d, d), lambda l, j: (l, 0, j)),
            pl.BlockSpec((1, 1, d), lambda l, j: (l, 0, j)),
        ],
        out_specs=pl.BlockSpec((1, b, d), lambda l, j: (l, 0, j)),
        out_shape=jax.ShapeDtypeStruct((depth, b, nd), F32),
        name="ada_mod",
    )(c, w_ada, b_ada.reshape(depth, 1, nd))


def _front_kernel(x_ref, mod_ref, nrm_ref, w_in_ref, wpool_ref, pscale_ref, sgn_ref, wcat_ref,
                  sbias_ref, mnp_ref, mng_ref, q_ref, k_ref, v_ref, pg_ref, e_ref, *, tm, sbw, pw):
    s = pl.program_id(1)
    sub = FRONT_SUB
    c0 = 3 * sbw
    c1 = c0 + pw
    heads_per_group = LANES // HEAD_DIM
    gdim = pw // len(POOL_WINDOWS)
    hd = pw // SG_HEADS
    lane_g = lax.broadcasted_iota(jnp.int32, (sub, LANES), 1)
    lane_e = lax.broadcasted_iota(jnp.int32, (sub + POOL_HALO, pw), 1)
    lane = lax.broadcasted_iota(jnp.int32, (sub, pw), 1)
    row = lax.broadcasted_iota(jnp.int32, (sub, pw), 0)
    lane_c = lax.broadcasted_iota(jnp.int32, (CHUNK, pw), 1)
    wlen = jnp.where(lane < gdim, POOL_WINDOWS[0],
                     jnp.where(lane < 2 * gdim, POOL_WINDOWS[1],
                               jnp.where(lane < 3 * gdim, POOL_WINDOWS[2], POOL_WINDOWS[3])))

    @pl.when(s == 0)
    def _():
        e_ref[0:POOL_HALO, :] = jnp.zeros((POOL_HALO, pw), F32)

    @pl.when(s > 0)
    def _():
        e_ref[0:POOL_HALO, :] = e_ref[tm:tm + POOL_HALO, :]

    def normed(r):
        x = x_ref[0, r * sub:(r + 1) * sub, :]
        h = _rms(x, nrm_ref[...]) * (1.0 + mod_ref[0, 1:2, :]) + mod_ref[0, 0:1, :]
        return h.astype(BF16)

    def project(r, hb):
        rows = slice(r * sub, (r + 1) * sub)
        for t, ref in enumerate((q_ref, k_ref, v_ref)):
            y = jnp.dot(hb, w_in_ref[:, t * sbw:(t + 1) * sbw], preferred_element_type=F32)
            for j in range(sbw // LANES):
                grp = y[:, j * LANES:(j + 1) * LANES]
                for hh in range(heads_per_group):
                    own = (lane_g >= hh * HEAD_DIM) & (lane_g < (hh + 1) * HEAD_DIM)
                    fill = jnp.where(lane_g == _spare_lane(hh), 1.0, 0.0) if t == 0 else 0.0
                    ref[0, heads_per_group * j + hh, rows, :] = jnp.where(own, grp, fill).astype(BF16)
        return jnp.dot(hb, w_in_ref[:, c0:], preferred_element_type=F32)

    def pool(r, p):
        e_ref[POOL_HALO + r * sub:POOL_HALO + (r + 1) * sub, :] = p
        e = e_ref[r * sub:(r + 1) * sub + POOL_HALO, :]
        s2 = e + pltpu.roll(e, 1, 0)
        s4 = s2 + pltpu.roll(s2, 2, 0)
        s8 = s4 + pltpu.roll(s4, 4, 0)
        s16 = s8 + pltpu.roll(s8, 8, 0)
        wsum = jnp.where(lane_e < gdim, s2,
                         jnp.where(lane_e < 2 * gdim, s4, jnp.where(lane_e < 3 * gdim, s8, s16)))
        wsum = wsum[POOL_HALO:, :]
        count = jnp.minimum(s * tm + r * sub + row + 1, wlen).astype(F32)
        pooled = wsum / count - p
        mixed = jnp.dot(pooled.astype(BF16), wpool_ref[...], preferred_element_type=F32) * pscale_ref[...]
        pg_ref[0, r * sub:(r + 1) * sub, 0:pw] = _rms(mixed, mnp_ref[...]).astype(BF16)

    def gate(r, u, vg):
        ug = jax.nn.gelu(u)
        vge = jax.nn.gelu(vg)
        mu = jnp.mean(vge, axis=-1, keepdims=True)
        var = jnp.mean(jnp.square(vge - mu), axis=-1, keepdims=True)
        vn = ((vge - mu) * lax.rsqrt(var + EPS) * sgn_ref[...]).astype(BF16)
        zero = jnp.zeros((CHUNK, pw), BF16)
        outs = []
        for c in range(sub // CHUNK):
            vc = vn[c * CHUNK:(c + 1) * CHUNK, :]
            stacked = jnp.concatenate(
                [jnp.where((lane_c >= hh * hd) & (lane_c < (hh + 1) * hd), vc, zero) for hh in range(SG_HEADS)],
                axis=0)
            outs.append(jnp.dot(wcat_ref[...], stacked, preferred_element_type=F32) + sbias_ref[...])
        g_out = ug * jnp.concatenate(outs, axis=0)
        pg_ref[0, r * sub:(r + 1) * sub, pw:2 * pw] = _rms(g_out, mng_ref[...]).astype(BF16)

    n_sub = tm // sub
    hbs = [normed(r) for r in range(n_sub)]
    mixers = [project(r, hbs[r]) for r in range(n_sub)]
    for r in range(n_sub):
        pool(r, mixers[r][:, :pw])
        gate(r, mixers[r][:, pw:2 * pw], mixers[r][:, 2 * pw:])


def _front(x, mod, nrm, w_in_b, wpool_bd, pscale, sgn, wcat, sbias, mnp, mng, *, tm, sbw, pw):
    b, s, d = x.shape
    nh = sbw // HEAD_DIM
    in_w = w_in_b.shape[1]
    const = lambda *shape: pl.BlockSpec(shape, lambda bi, si: (0,) * len(shape))
    qkv_spec = pl.BlockSpec((1, nh, tm, LANES), lambda bi, si: (bi, 0, si, 0))
    qkv_shape = jax.ShapeDtypeStruct((b, nh, s, LANES), BF16)
    return pl.pallas_call(
        functools.partial(_front_kernel, tm=tm, sbw=sbw, pw=pw),
        grid=(b, s // tm),
        in_specs=[
            pl.BlockSpec((1, tm, d), lambda bi, si: (bi, si, 0)),
            pl.BlockSpec((1, N_MOD, d), lambda bi, si: (bi, 0, 0)),
            const(1, d), const(d, in_w), const(pw, pw), const(1, pw), const(1, pw),
            const(CHUNK, SG_HEADS * CHUNK), const(CHUNK, pw), const(1, pw), const(1, pw),
        ],
        out_specs=[qkv_spec, qkv_spec, qkv_spec,
                   pl.BlockSpec((1, tm, 2 * pw), lambda bi, si: (bi, si, 0))],
        out_shape=[qkv_shape, qkv_shape, qkv_shape, jax.ShapeDtypeStruct((b, s, 2 * pw), BF16)],
        scratch_shapes=[pltpu.VMEM((tm + POOL_HALO, pw), F32)],
        compiler_params=pltpu.CompilerParams(
            dimension_semantics=("arbitrary", "arbitrary"), vmem_limit_bytes=VMEM_LIMIT),
        name="front",
    )(x, mod, nrm, w_in_b, wpool_bd, pscale, sgn, wcat, sbias, mnp, mng)


TRI_WIDTH = 256


def _neg_abs(z):
    sign = jnp.uint32(0x80000000)
    return lax.bitcast_convert_type(lax.bitcast_convert_type(z, jnp.uint32) | sign, F32)


def _attn_kernel(q_ref, kp_ref, kc_ref, vp_ref, vc_ref, pg_ref, x_ref, mod_ref, lc_ref, tri_ref, mna_ref,
                 wout_ref, o_ref, *, tq, nh):
    i = pl.program_id(1)
    nqb = tq // Q_BLOCK
    edge = KV_SPAN - Q_BLOCK
    group = LANES // HEAD_DIM
    assert KV_SPAN - 2 * Q_BLOCK + Q_BLOCK == 2 * TRI_WIDTH
    kb_prev = jnp.where(i == 0, NEG_BIG, 0.0).astype(BF16)
    prev_bias = [kb_prev * lc_ref[hh:hh + 1, :] for hh in range(group)]

    def window(prev_ref, cur_ref, h, off):
        return prev_ref[0, h, off:, :], cur_ref[0, h, 0:off + Q_BLOCK, :]

    newest = (lax.broadcasted_iota(jnp.int32, (Q_BLOCK, Q_BLOCK), 1)
              < lax.broadcasted_iota(jnp.int32, (Q_BLOCK, Q_BLOCK), 0))
    half = TRI_WIDTH - Q_BLOCK

    def scores(item):
        qb, h = divmod(item, nh)
        off = qb * Q_BLOCK
        k_prev, k_cur = window(kp_ref, kc_ref, h, off)
        keys = jnp.concatenate([k_prev + prev_bias[h % group], k_cur], axis=0)
        z = lax.dot_general(q_ref[0, h, off:off + Q_BLOCK, :], keys, (((1,), (1,)), ((), ())),
                            preferred_element_type=F32)
        return jnp.concatenate([jnp.where(newest, z[:, edge:], z[:, :Q_BLOCK]), z[:, Q_BLOCK:edge]], axis=1)

    def softplus(z):
        n = jnp.maximum(z, 0.0) + jnp.log(1.0 + jnp.exp2(_neg_abs(z))) * LOG2E
        nb = n.astype(BF16)
        n_new = jnp.where(newest, n[:, :Q_BLOCK], 0.0)
        nb_new = n_new.astype(BF16)
        nb_old = nb[:, :Q_BLOCK] - nb_new
        tot_new = jnp.sum(n_new, axis=-1, keepdims=True)
        tot_mid = jnp.sum(n[:, Q_BLOCK + half:], axis=-1, keepdims=True)
        chunks = (jnp.concatenate([nb_old, nb[:, Q_BLOCK:Q_BLOCK + half]], axis=1), nb[:, Q_BLOCK + half:], nb_new)
        return z - n, chunks, (tot_mid + tot_new, tot_new)

    def log_weight(state):
        dmn, (c_first, c_mid, c_new), (tot_first, tot_mid) = state
        l_first = jnp.dot(c_first, tri_ref[...], preferred_element_type=F32) + tot_first
        l_mid = jnp.dot(c_mid, tri_ref[...], preferred_element_type=F32) + tot_mid
        l_new = jnp.dot(c_new, tri_ref[0:Q_BLOCK, 0:Q_BLOCK], preferred_element_type=F32)
        later = jnp.concatenate([jnp.where(newest, l_new, l_first[:, :Q_BLOCK]), l_first[:, Q_BLOCK:], l_mid],
                                axis=1)
        return dmn - later

    def weights(arg):
        w = jnp.exp2(arg)
        wb = w.astype(BF16)
        w_new = jnp.where(newest, w[:, :Q_BLOCK], 0.0).astype(BF16)
        return jnp.concatenate([wb[:, :Q_BLOCK] - w_new, wb[:, Q_BLOCK:], w_new], axis=1)

    def weighted_values(item, w):
        qb, h = divmod(item, nh)
        values = jnp.concatenate(window(vp_ref, vc_ref, h, qb * Q_BLOCK), axis=0)
        return jnp.dot(w, values, preferred_element_type=F32)

    def project(qb, pvs):
        a = jnp.concatenate([sum(pvs[g * group + 1:(g + 1) * group], pvs[g * group])
                             for g in range(nh // group)], axis=1)
        rows = slice(qb * Q_BLOCK, (qb + 1) * Q_BLOCK)
        an = _rms(a, mna_ref[...]).astype(BF16)
        merged = jnp.concatenate([an, pg_ref[0, rows, :]], axis=1)
        y = jnp.dot(merged, wout_ref[...], preferred_element_type=F32)
        o_ref[0, rows, :] = x_ref[0, rows, :] + mod_ref[0, 2:3, :] * y

    zs = [scores(h) for h in range(nh)]
    for qb in range(nqb):
        args = [log_weight(softplus(z)) for z in zs]
        if qb + 1 < nqb:
            zs = [scores((qb + 1) * nh + h) for h in range(nh)]
        project(qb, [weighted_values(qb * nh + h, weights(args[h])) for h in range(nh)])


def _attn(q, k, v, pg, x, mod, lane_consts, tri, mna, wout_b, *, tq):
    b, s, d = x.shape
    nh = q.shape[1]
    mixw = wout_b.shape[0]
    const = lambda *shape: pl.BlockSpec(shape, lambda bi, si: (0,) * len(shape))
    cur = pl.BlockSpec((1, nh, tq, LANES), lambda bi, si: (bi, 0, si, 0))
    prev = pl.BlockSpec((1, nh, tq, LANES), lambda bi, si: (bi, 0, jnp.maximum(si - 1, 0), 0))
    return pl.pallas_call(
        functools.partial(_attn_kernel, tq=tq, nh=nh),
        grid=(b, s // tq),
        in_specs=[
            cur, prev, cur, prev, cur,
            pl.BlockSpec((1, tq, pg.shape[2]), lambda bi, si: (bi, si, 0)),
            pl.BlockSpec((1, tq, d), lambda bi, si: (bi, si, 0)),
            pl.BlockSpec((1, N_MOD, d), lambda bi, si: (bi, 0, 0)),
            const(*lane_consts.shape), const(*tri.shape),
            const(1, nh * HEAD_DIM), const(mixw, d),
        ],
        out_specs=pl.BlockSpec((1, tq, d), lambda bi, si: (bi, si, 0)),
        out_shape=jax.ShapeDtypeStruct((b, s, d), F32),
        compiler_params=pltpu.CompilerParams(
            dimension_semantics=("arbitrary", "arbitrary"), vmem_limit_bytes=VMEM_LIMIT),
        name="attn",
    )(q, k, k, v, v, pg, x, mod, lane_consts, tri, mna, wout_b)


def _ffn_kernel(x_ref, mod_ref, nrm_ref, wg_ref, wu_ref, wd_ref, fn_ref, o_ref, *, final):
    sub = FFN_SUB
    subs = [slice(r * sub, (r + 1) * sub) for r in range(x_ref.shape[1] // sub)]
    hbs = []
    for rows in subs:
        h = _rms(x_ref[0, rows, :], nrm_ref[...]) * (1.0 + mod_ref[0, 4:5, :]) + mod_ref[0, 3:4, :]
        hbs.append(h.astype(BF16))
    gate_up = [(jnp.dot(hb, wg_ref[...], preferred_element_type=F32),
                jnp.dot(hb, wu_ref[...], preferred_element_type=F32)) for hb in hbs]
    for rows, (gate, up) in zip(subs, gate_up):
        act = (gate / (1.0 + jnp.exp(-gate)) * up).astype(BF16)
        y = jnp.dot(act, wd_ref[...], preferred_element_type=F32)
        out = x_ref[0, rows, :] + mod_ref[0, 5:6, :] * y
        if final:
            out = _rms(out, fn_ref[...])
        o_ref[0, rows, :] = out


def _ffn(x, mod, nrm, wg_b, wu_b, wd_b, fn, *, tm, final):
    b, s, d = x.shape
    dff = wg_b.shape[1]
    once = lambda *shape: pl.BlockSpec(shape, lambda bi, si: (0,) * len(shape),
                                       pipeline_mode=pl.Buffered(1))
    return pl.pallas_call(
        functools.partial(_ffn_kernel, final=final),
        grid=(b, s // tm),
        in_specs=[
            pl.BlockSpec((1, tm, d), lambda bi, si: (bi, si, 0)),
            pl.BlockSpec((1, N_MOD, d), lambda bi, si: (bi, 0, 0)),
            once(1, d), once(d, dff), once(d, dff), once(dff, d), once(1, d),
        ],
        out_specs=pl.BlockSpec((1, tm, d), lambda bi, si: (bi, si, 0)),
        out_shape=jax.ShapeDtypeStruct((b, s, d), F32),
        compiler_params=pltpu.CompilerParams(
            dimension_semantics=("arbitrary", "arbitrary"), vmem_limit_bytes=VMEM_LIMIT),
        name="ffn",
    )(x, mod, nrm, wg_b, wu_b, wd_b, fn)


def _lane_constants():
    rows = np.zeros((16, LANES), np.float32)
    for hh in range(LANES // HEAD_DIM):
        rows[hh, _spare_lane(hh)] = 1.0
    return rows


def _suffix_sum_matrix(n):
    j = np.arange(n)
    return (j[:, None] > j[None, :]).astype(np.float32)


def kernel(x, c, w_ada, b_ada, norm_mix_in, w_in, w_pool, pool_scale, sg_norm, w_s, b_s, mix_norm, w_out,
           norm_ffn_in, w_gate_up, w_down, final_norm):
    b, s, d = x.shape
    depth = w_ada.shape[0]
    pw = pool_scale.shape[1]
    sbw = (w_in.shape[2] - 3 * pw) // 3
    dff = w_down.shape[1]
    tm = min(ROW_TILE, s)
    tf = min(FRONT_TILE, s)
    tw = min(FFN_TILE, s)
    assert s % tm == 0 and tm % Q_BLOCK == 0 and tm >= SB_WINDOW and sbw % LANES == 0
    assert s % tf == 0 and tf % FRONT_SUB == 0 and FRONT_SUB % CHUNK == 0
    assert s % tw == 0 and tw % FFN_SUB == 0

    mod_all = _ada(c, w_ada, b_ada).reshape(depth, b, N_MOD, d)

    q_scale = jnp.concatenate([jnp.full((sbw,), LOG2E * HEAD_DIM ** -0.5, F32),
                               jnp.ones((w_in.shape[2] - sbw,), F32)])
    w_in_b = (w_in * q_scale).astype(BF16)
    groups = w_pool.shape[1]
    gdim = w_pool.shape[2]
    eye = jnp.eye(groups, dtype=F32)
    wpool_bd = (w_pool[:, :, :, None, :] * eye[None, :, None, :, None]).reshape(
        depth, groups * gdim, groups * gdim).astype(BF16)
    causal = jnp.tril(jnp.ones((CHUNK, CHUNK), dtype=bool))
    ws_c = jnp.where(causal[None, None], w_s, 0.0)
    wcat = ws_c.transpose(0, 2, 1, 3).reshape(depth, CHUNK, SG_HEADS * CHUNK).astype(BF16)
    sbias = jnp.repeat(b_s.transpose(0, 2, 1), pw // SG_HEADS, axis=2)
    w_out_b = w_out.astype(BF16)
    wg_b = w_gate_up[:, :, :dff].astype(BF16)
    wu_b = w_gate_up[:, :, dff:].astype(BF16)
    wd_b = w_down.astype(BF16)
    lane_consts = jnp.asarray(_lane_constants(), dtype=BF16)
    tri = jnp.asarray(_suffix_sum_matrix(TRI_WIDTH), dtype=BF16)
    row = lambda v: v.reshape(1, -1)

    for l in range(depth):
        mod = mod_all[l]
        q, k, v, pg = _front(x, mod, row(norm_mix_in[l]), w_in_b[l], wpool_bd[l], row(pool_scale[l]),
                             row(sg_norm[l]), wcat[l], sbias[l], row(mix_norm[l, sbw:sbw + pw]),
                             row(mix_norm[l, sbw + pw:]), tm=tf, sbw=sbw, pw=pw)
        x = _attn(q, k, v, pg, x, mod, lane_consts, tri, row(mix_norm[l, :sbw]), w_out_b[l], tq=tm)
        x = _ffn(x, mod, row(norm_ffn_in[l]), wg_b[l], wu_b[l], wd_b[l], row(final_norm),
                 tm=tw, final=(l == depth - 1))
    return x
```

```python
import functools

import numpy as np
import jax
import jax.numpy as jnp
from jax import lax
from jax.experimental import pallas as pl
from jax.experimental.pallas import tpu as pltpu

F32 = jnp.float32
BF16 = jnp.bfloat16

HEAD_DIM = 64
Q_BLOCK = 128
SB_WINDOW = 512
KV_SPAN = SB_WINDOW + Q_BLOCK
POOL_WINDOWS = (2, 4, 8, 16)
POOL_HALO = 16
CHUNK = 128
SG_HEADS = 4
N_MOD = 6
EPS = 1e-6
LANES = 128
NEG_BIG = -1e30
LOG2E = 1.4426950408889634

ROW_TILE = 512
FRONT_TILE = 1024
FRONT_SUB = 256
FFN_SUB = 256
FFN_COLS = 256
VMEM_LIMIT = 56 * 1024 * 1024


def _spare_lane(head_in_group):
    return HEAD_DIM if head_in_group == 0 else 0


def _rms(x, gain):
    return x * lax.rsqrt(jnp.mean(x * x, axis=-1, keepdims=True) + EPS) * gain


def _ada_kernel(c_ref, w_ref, b_ref, o_ref):
    c = c_ref[...]
    c_act = c / (1.0 + jnp.exp(-c))
    o_ref[0] = jnp.dot(c_act, w_ref[0], preferred_element_type=F32,
                       precision=lax.Precision.HIGHEST) + b_ref[0]


def _ada(c, w_ada, b_ada):
    depth, d, nd = w_ada.shape
    b = c.shape[0]
    return pl.pallas_call(
        _ada_kernel,
        grid=(depth, nd // d),
        in_specs=[
            pl.BlockSpec((b, d), lambda l, j: (0, 0)),
            pl.BlockSpec((1, d, d), lambda l, j: (l, 0, j)),
            pl.BlockSpec((1, 1, d), lambda l, j: (l, 0, j)),
        ],
        out_specs=pl.BlockSpec((1, b, d), lambda l, j: (l, 0, j)),
        out_shape=jax.ShapeDtypeStruct((depth, b, nd), F32),
        name="ada_mod",
    )(c, w_ada, b_ada.reshape(depth, 1, nd))


def _front_kernel(x_ref, mod_ref, nrm_ref, w_in_ref, wpool_ref, pscale_ref, sgn_ref, wcat_ref,
                  sbias_ref, mnp_ref, mng_ref, q_ref, k_ref, v_ref, pg_ref, e_ref, *, tm, sbw, pw):
    s = pl.program_id(1)
    sub = FRONT_SUB
    c0 = 3 * sbw
    c1 = c0 + pw
    heads_per_group = LANES // HEAD_DIM
    gdim = pw // len(POOL_WINDOWS)
    hd = pw // SG_HEADS
    lane_g = lax.broadcasted_iota(jnp.int32, (sub, LANES), 1)
    lane_e = lax.broadcasted_iota(jnp.int32, (sub + POOL_HALO, pw), 1)
    lane = lax.broadcasted_iota(jnp.int32, (sub, pw), 1)
    row = lax.broadcasted_iota(jnp.int32, (sub, pw), 0)
    lane_c = lax.broadcasted_iota(jnp.int32, (CHUNK, pw), 1)
    wlen = jnp.where(lane < gdim, POOL_WINDOWS[0],
                     jnp.where(lane < 2 * gdim, POOL_WINDOWS[1],
                               jnp.where(lane < 3 * gdim, POOL_WINDOWS[2], POOL_WINDOWS[3])))

    @pl.when(s == 0)
    def _():
        e_ref[0:POOL_HALO, :] = jnp.zeros((POOL_HALO, pw), F32)

    @pl.when(s > 0)
    def _():
        e_ref[0:POOL_HALO, :] = e_ref[tm:tm + POOL_HALO, :]

    def normed(r):
        x = x_ref[0, r * sub:(r + 1) * sub, :]
        h = _rms(x, nrm_ref[...]) * (1.0 + mod_ref[0, 1:2, :]) + mod_ref[0, 0:1, :]
        return h.astype(BF16)

    def project(r, hb):
        rows = slice(r * sub, (r + 1) * sub)
        for t, ref in enumerate((q_ref, k_ref, v_ref)):
            y = jnp.dot(hb, w_in_ref[:, t * sbw:(t + 1) * sbw], preferred_element_type=F32)
            for j in range(sbw // LANES):
                grp = y[:, j * LANES:(j + 1) * LANES]
                for hh in range(heads_per_group):
                    own = (lane_g >= hh * HEAD_DIM) & (lane_g < (hh + 1) * HEAD_DIM)
                    fill = jnp.where(lane_g == _spare_lane(hh), 1.0, 0.0) if t == 0 else 0.0
                    ref[0, heads_per_group * j + hh, rows, :] = jnp.where(own, grp, fill).astype(BF16)
        return jnp.dot(hb, w_in_ref[:, c0:], preferred_element_type=F32)

    def pool(r, p):
        e_ref[POOL_HALO + r * sub:POOL_HALO + (r + 1) * sub, :] = p
        e = e_ref[r * sub:(r + 1) * sub + POOL_HALO, :]
        s2 = e + pltpu.roll(e, 1, 0)
        s4 = s2 + pltpu.roll(s2, 2, 0)
        s8 = s4 + pltpu.roll(s4, 4, 0)
        s16 = s8 + pltpu.roll(s8, 8, 0)
        wsum = jnp.where(lane_e < gdim, s2,
                         jnp.where(lane_e < 2 * gdim, s4, jnp.where(lane_e < 3 * gdim, s8, s16)))
        wsum = wsum[POOL_HALO:, :]
        count = jnp.minimum(s * tm + r * sub + row + 1, wlen).astype(F32)
        pooled = wsum / count - p
        mixed = jnp.dot(pooled.astype(BF16), wpool_ref[...], preferred_element_type=F32) * pscale_ref[...]
        pg_ref[0, r * sub:(r + 1) * sub, 0:pw] = _rms(mixed, mnp_ref[...]).astype(BF16)

    def gate(r, u, vg):
        ug = jax.nn.gelu(u)
        vge = jax.nn.gelu(vg)
        mu = jnp.mean(vge, axis=-1, keepdims=True)
        var = jnp.mean(jnp.square(vge - mu), axis=-1, keepdims=True)
        vn = ((vge - mu) * lax.rsqrt(var + EPS) * sgn_ref[...]).astype(BF16)
        zero = jnp.zeros((CHUNK, pw), BF16)
        outs = []
        for c in range(sub // CHUNK):
            vc = vn[c * CHUNK:(c + 1) * CHUNK, :]
            stacked = jnp.concatenate(
                [jnp.where((lane_c >= hh * hd) & (lane_c < (hh + 1) * hd), vc, zero) for hh in range(SG_HEADS)],
                axis=0)
            outs.append(jnp.dot(wcat_ref[...], stacked, preferred_element_type=F32) + sbias_ref[...])
        g_out = ug * jnp.concatenate(outs, axis=0)
        pg_ref[0, r * sub:(r + 1) * sub, pw:2 * pw] = _rms(g_out, mng_ref[...]).astype(BF16)

    n_sub = tm // sub
    hbs = [normed(r) for r in range(n_sub)]
    mixers = [project(r, hbs[r]) for r in range(n_sub)]
    for r in range(n_sub):
        pool(r, mixers[r][:, :pw])
        gate(r, mixers[r][:, pw:2 * pw], mixers[r][:, 2 * pw:])


def _front(x, mod, nrm, w_in_b, wpool_bd, pscale, sgn, wcat, sbias, mnp, mng, *, tm, sbw, pw):
    b, s, d = x.shape
    nh = sbw // HEAD_DIM
    in_w = w_in_b.shape[1]
    const = lambda *shape: pl.BlockSpec(shape, lambda bi, si: (0,) * len(shape))
    qkv_spec = pl.BlockSpec((1, nh, tm, LANES), lambda bi, si: (bi, 0, si, 0))
    qkv_shape = jax.ShapeDtypeStruct((b, nh, s, LANES), BF16)
    return pl.pallas_call(
        functools.partial(_front_kernel, tm=tm, sbw=sbw, pw=pw),
        grid=(b, s // tm),
        in_specs=[
            pl.BlockSpec((1, tm, d), lambda bi, si: (bi, si, 0)),
            pl.BlockSpec((1, N_MOD, d), lambda bi, si: (bi, 0, 0)),
            const(1, d), const(d, in_w), const(pw, pw), const(1, pw), const(1, pw),
            const(CHUNK, SG_HEADS * CHUNK), const(CHUNK, pw), const(1, pw), const(1, pw),
        ],
        out_specs=[qkv_spec, qkv_spec, qkv_spec,
                   pl.BlockSpec((1, tm, 2 * pw), lambda bi, si: (bi, si, 0))],
        out_shape=[qkv_shape, qkv_shape, qkv_shape, jax.ShapeDtypeStruct((b, s, 2 * pw), BF16)],
        scratch_shapes=[pltpu.VMEM((tm + POOL_HALO, pw), F32)],
        compiler_params=pltpu.CompilerParams(
            dimension_semantics=("arbitrary", "arbitrary"), vmem_limit_bytes=VMEM_LIMIT),
        name="front",
    )(x, mod, nrm, w_in_b, wpool_bd, pscale, sgn, wcat, sbias, mnp, mng)


TRI_WIDTH = 256


def _neg_abs(z):
    sign = jnp.uint32(0x80000000)
    return lax.bitcast_convert_type(lax.bitcast_convert_type(z, jnp.uint32) | sign, F32)


def _mixer_ffn_kernel(q_ref, kp_ref, kc_ref, vp_ref, vc_ref, pg_ref, x_ref, moda_ref, modf_ref, lc_ref, tri_ref,
                      mna_ref, wout_ref, nrm_ref, wg_ref, wu_ref, wd_ref, fn_ref, o_ref, x1_scr, res_scr,
                      *, tq, nh, n_seq, n_tiles, final):
    j = pl.program_id(0)
    i = jnp.minimum(j, n_tiles - 1) % n_seq
    nqb = tq // Q_BLOCK
    edge = KV_SPAN - Q_BLOCK
    group = LANES // HEAD_DIM
    assert KV_SPAN - 2 * Q_BLOCK + Q_BLOCK == 2 * TRI_WIDTH

    @pl.when(j == 0)
    def _():
        x1_scr[...] = jnp.zeros(x1_scr.shape, F32)

    res_scr[...] = x1_scr[...]

    sub = FFN_SUB
    subs = [slice(r * sub, (r + 1) * sub) for r in range(tq // sub)]
    dff = wg_ref.shape[1]
    d_model = wd_ref.shape[1]
    n_ct = dff // FFN_COLS
    hbs, acts = {}, {}

    def ffn_norm(r):
        h = _rms(res_scr[subs[r], :], nrm_ref[...]) * (1.0 + modf_ref[0, 4:5, :]) + modf_ref[0, 3:4, :]
        hbs[r] = h.astype(BF16)

    def ffn_gate_up(r, c):
        cols = slice(c * FFN_COLS, (c + 1) * FFN_COLS)
        gate = jnp.dot(hbs[r], wg_ref[:, cols], preferred_element_type=F32)
        up = jnp.dot(hbs[r], wu_ref[:, cols], preferred_element_type=F32)
        acts[r, c] = (gate / (1.0 + jnp.exp(-gate)) * up).astype(BF16)

    def ffn_down(r, c):
        cols = slice(c * FFN_COLS, (c + 1) * FFN_COLS)
        act = jnp.concatenate([acts[r, k] for k in range(n_ct)], axis=1)
        y = jnp.dot(act, wd_ref[:, cols], preferred_element_type=F32)
        out = res_scr[subs[r], cols] + modf_ref[0, 5:6, cols] * y
        o_ref[0, subs[r], cols] = out

    def ffn_finish(r):
        if final:
            o_ref[0, subs[r], :] = _rms(o_ref[0, subs[r], :], fn_ref[...])

    pieces = []
    for r in range(len(subs)):
        pieces += [functools.partial(ffn_gate_up, r, c) for c in range(n_ct)]
        pieces += [functools.partial(ffn_down, r, c) for c in range(d_model // FFN_COLS)]
        pieces.append(functools.partial(ffn_finish, r))

    kb_prev = jnp.where(i == 0, NEG_BIG, 0.0).astype(BF16)
    prev_bias = [kb_prev * lc_ref[hh:hh + 1, :] for hh in range(group)]

    def window(prev_ref, cur_ref, h, off):
        return prev_ref[0, h, off:, :], cur_ref[0, h, 0:off + Q_BLOCK, :]

    newest = (lax.broadcasted_iota(jnp.int32, (Q_BLOCK, Q_BLOCK), 1)
              < lax.broadcasted_iota(jnp.int32, (Q_BLOCK, Q_BLOCK), 0))
    half = TRI_WIDTH - Q_BLOCK

    def scores(item):
        qb, h = divmod(item, nh)
        off = qb * Q_BLOCK
        k_prev, k_cur = window(kp_ref, kc_ref, h, off)
        keys = jnp.concatenate([k_prev + prev_bias[h % group], k_cur], axis=0)
        z = lax.dot_general(q_ref[0, h, off:off + Q_BLOCK, :], keys, (((1,), (1,)), ((), ())),
                            preferred_element_type=F32)
        return jnp.concatenate([jnp.where(newest, z[:, edge:], z[:, :Q_BLOCK]), z[:, Q_BLOCK:edge]], axis=1)

    def softplus(z):
        n = jnp.maximum(z, 0.0) + jnp.log(1.0 + jnp.exp2(_neg_abs(z))) * LOG2E
        nb = n.astype(BF16)
        n_new = jnp.where(newest, n[:, :Q_BLOCK], 0.0)
        nb_new = n_new.astype(BF16)
        nb_old = nb[:, :Q_BLOCK] - nb_new
        tot_new = jnp.sum(n_new, axis=-1, keepdims=True)
        tot_mid = jnp.sum(n[:, Q_BLOCK + half:], axis=-1, keepdims=True)
        chunks = (jnp.concatenate([nb_old, nb[:, Q_BLOCK:Q_BLOCK + half]], axis=1), nb[:, Q_BLOCK + half:], nb_new)
        return z - n, chunks, (tot_mid + tot_new, tot_new)

    def log_weight(state):
        dmn, (c_first, c_mid, c_new), (tot_first, tot_mid) = state
        l_first = jnp.dot(c_first, tri_ref[...], preferred_element_type=F32) + tot_first
        l_mid = jnp.dot(c_mid, tri_ref[...], preferred_element_type=F32) + tot_mid
        l_new = jnp.dot(c_new, tri_ref[0:Q_BLOCK, 0:Q_BLOCK], preferred_element_type=F32)
        later = jnp.concatenate([jnp.where(newest, l_new, l_first[:, :Q_BLOCK]), l_first[:, Q_BLOCK:], l_mid],
                                axis=1)
        return dmn - later

    def weights(arg):
        w = jnp.exp2(arg)
        wb = w.astype(BF16)
        w_new = jnp.where(newest, w[:, :Q_BLOCK], 0.0).astype(BF16)
        return jnp.concatenate([wb[:, :Q_BLOCK] - w_new, wb[:, Q_BLOCK:], w_new], axis=1)

    def weighted_values(item, w):
        qb, h = divmod(item, nh)
        values = jnp.concatenate(window(vp_ref, vc_ref, h, qb * Q_BLOCK), axis=0)
        return jnp.dot(w, values, preferred_element_type=F32)

    def project(qb, pvs):
        a = jnp.concatenate([sum(pvs[g * group + 1:(g + 1) * group], pvs[g * group])
                             for g in range(nh // group)], axis=1)
        rows = slice(qb * Q_BLOCK, (qb + 1) * Q_BLOCK)
        an = _rms(a, mna_ref[...]).astype(BF16)
        merged = jnp.concatenate([an, pg_ref[0, rows, :]], axis=1)
        y = jnp.dot(merged, wout_ref[...], preferred_element_type=F32)
        x1_scr[rows, :] = x_ref[0, rows, :] + moda_ref[0, 2:3, :] * y

    n_items = nqb * nh
    for r in range(len(subs)):
        ffn_norm(r)
    per_step = -(-len(pieces) // n_items)
    zs, states, args, ws, pvs = {}, {}, {}, {}, {}
    for step in range(-2, n_items + 3):
        for _ in range(per_step):
            if pieces and step >= 0:
                pieces.pop(0)()
        if 0 <= step + 2 < n_items:
            zs[step + 2] = scores(step + 2)
        if 0 <= step < n_items:
            states[step] = softplus(zs.pop(step))
        if 0 <= step - 1 < n_items:
            args[step - 1] = log_weight(states.pop(step - 1))
        if 0 <= step - 2 < n_items:
            ws[step - 2] = weights(args.pop(step - 2))
        if 0 <= step - 3 < n_items:
            item = step - 3
            pvs[item] = weighted_values(item, ws.pop(item))
            if item % nh == nh - 1:
                qb = item // nh
                project(qb, [pvs.pop(qb * nh + h) for h in range(nh)])
    assert not pieces


def _mixer_ffn(q, k, v, pg, x, mod, lane_consts, tri, mna, wout_b, nrm, wg_b, wu_b, wd_b, fn, *, tq, final):
    b, s, d = x.shape
    nh = q.shape[1]
    mixw = wout_b.shape[0]
    dff = wg_b.shape[1]
    n_seq = s // tq
    n_tiles = b * n_seq
    once = lambda *shape: pl.BlockSpec(shape, lambda j: (0,) * len(shape), pipeline_mode=pl.Buffered(1))
    mixer_tile = lambda j: jnp.minimum(j, n_tiles - 1)
    ffn_tile = lambda j: jnp.maximum(j - 1, 0)
    cur = pl.BlockSpec((1, nh, tq, LANES), lambda j: (mixer_tile(j) // n_seq, 0, mixer_tile(j) % n_seq, 0))
    prev = pl.BlockSpec((1, nh, tq, LANES),
                        lambda j: (mixer_tile(j) // n_seq, 0, jnp.maximum(mixer_tile(j) % n_seq - 1, 0), 0))
    rows_a = lambda width: pl.BlockSpec((1, tq, width),
                                        lambda j: (mixer_tile(j) // n_seq, mixer_tile(j) % n_seq, 0))
    return pl.pallas_call(
        functools.partial(_mixer_ffn_kernel, tq=tq, nh=nh, n_seq=n_seq, n_tiles=n_tiles, final=final),
        grid=(n_tiles + 1,),
        in_specs=[
            cur, prev, cur, prev, cur, rows_a(pg.shape[2]), rows_a(d),
            pl.BlockSpec((1, N_MOD, d), lambda j: (mixer_tile(j) // n_seq, 0, 0)),
            pl.BlockSpec((1, N_MOD, d), lambda j: (ffn_tile(j) // n_seq, 0, 0)),
            once(*lane_consts.shape), once(*tri.shape), once(1, nh * HEAD_DIM), once(mixw, d),
            once(1, d), once(d, dff), once(d, dff), once(dff, d), once(1, d),
        ],
        out_specs=pl.BlockSpec((1, tq, d), lambda j: (ffn_tile(j) // n_seq, ffn_tile(j) % n_seq, 0)),
        out_shape=jax.ShapeDtypeStruct((b, s, d), F32),
        scratch_shapes=[pltpu.VMEM((tq, d), F32), pltpu.VMEM((tq, d), F32)],
        compiler_params=pltpu.CompilerParams(dimension_semantics=("arbitrary",), vmem_limit_bytes=VMEM_LIMIT),
        name="mixer_ffn",
    )(q, k, k, v, v, pg, x, mod, mod, lane_consts, tri, mna, wout_b, nrm, wg_b, wu_b, wd_b, fn)


def _lane_constants():
    rows = np.zeros((16, LANES), np.float32)
    for hh in range(LANES // HEAD_DIM):
        rows[hh, _spare_lane(hh)] = 1.0
    return rows


def _suffix_sum_matrix(n):
    j = np.arange(n)
    return (j[:, None] > j[None, :]).astype(np.float32)


def kernel(x, c, w_ada, b_ada, norm_mix_in, w_in, w_pool, pool_scale, sg_norm, w_s, b_s, mix_norm, w_out,
           norm_ffn_in, w_gate_up, w_down, final_norm):
    b, s, d = x.shape
    depth = w_ada.shape[0]
    pw = pool_scale.shape[1]
    sbw = (w_in.shape[2] - 3 * pw) // 3
    dff = w_down.shape[1]
    tm = min(ROW_TILE, s)
    tf = min(FRONT_TILE, s)
    assert s % tm == 0 and tm % Q_BLOCK == 0 and tm >= SB_WINDOW and sbw % LANES == 0
    assert s % tf == 0 and tf % FRONT_SUB == 0 and FRONT_SUB % CHUNK == 0
    assert tm % FFN_SUB == 0

    mod_all = _ada(c, w_ada, b_ada).reshape(depth, b, N_MOD, d)

    q_scale = jnp.concatenate([jnp.full((sbw,), LOG2E * HEAD_DIM ** -0.5, F32),
                               jnp.ones((w_in.shape[2] - sbw,), F32)])
    w_in_b = (w_in * q_scale).astype(BF16)
    groups = w_pool.shape[1]
    gdim = w_pool.shape[2]
    eye = jnp.eye(groups, dtype=F32)
    wpool_bd = (w_pool[:, :, :, None, :] * eye[None, :, None, :, None]).reshape(
        depth, groups * gdim, groups * gdim).astype(BF16)
    causal = jnp.tril(jnp.ones((CHUNK, CHUNK), dtype=bool))
    ws_c = jnp.where(causal[None, None], w_s, 0.0)
    wcat = ws_c.transpose(0, 2, 1, 3).reshape(depth, CHUNK, SG_HEADS * CHUNK).astype(BF16)
    sbias = jnp.repeat(b_s.transpose(0, 2, 1), pw // SG_HEADS, axis=2)
    w_out_b = w_out.astype(BF16)
    wg_b = w_gate_up[:, :, :dff].astype(BF16)
    wu_b = w_gate_up[:, :, dff:].astype(BF16)
    wd_b = w_down.astype(BF16)
    lane_consts = jnp.asarray(_lane_constants(), dtype=BF16)
    tri = jnp.asarray(_suffix_sum_matrix(TRI_WIDTH), dtype=BF16)
    row = lambda v: v.reshape(1, -1)

    for l in range(depth):
        mod = mod_all[l]
        q, k, v, pg = _front(x, mod, row(norm_mix_in[l]), w_in_b[l], wpool_bd[l], row(pool_scale[l]),
                             row(sg_norm[l]), wcat[l], sbias[l], row(mix_norm[l, sbw:sbw + pw]),
                             row(mix_norm[l, sbw + pw:]), tm=tf, sbw=sbw, pw=pw)
        x = _mixer_ffn(q, k, v, pg, x, mod, lane_consts, tri, row(mix_norm[l, :sbw]), w_out_b[l],
                       row(norm_ffn_in[l]), wg_b[l], wu_b[l], wd_b[l], row(final_norm),
                       tq=tm, final=(l == depth - 1))
    return x
```

```python
import functools

import numpy as np
import jax
import jax.numpy as jnp
from jax import lax
from jax.experimental import pallas as pl
from jax.experimental.pallas import tpu as pltpu

F32 = jnp.float32
BF16 = jnp.bfloat16

HEAD_DIM = 64
Q_BLOCK = 128
SB_WINDOW = 512
KV_SPAN = SB_WINDOW + Q_BLOCK
POOL_WINDOWS = (2, 4, 8, 16)
POOL_HALO = 16
CHUNK = 128
SG_HEADS = 4
N_MOD = 6
EPS = 1e-6
LANES = 128
NEG_BIG = -1e30
LOG2E = 1.4426950408889634

ROW_TILE = 512
FRONT_TILE = 1024
FRONT_SUB = 256
FFN_SUB = 256
FFN_COLS = 256
VMEM_LIMIT = 56 * 1024 * 1024


def _spare_lane(head_in_group):
    return HEAD_DIM if head_in_group == 0 else 0


def _rms(x, gain):
    return x * lax.rsqrt(jnp.mean(x * x, axis=-1, keepdims=True) + EPS) * gain


def _ada_kernel(c_ref, w_ref, b_ref, o_ref):
    c = c_ref[...]
    c_act = c / (1.0 + jnp.exp(-c))
    o_ref[0] = jnp.dot(c_act, w_ref[0], preferred_element_type=F32,
                       precision=lax.Precision.HIGHEST) + b_ref[0]


def _ada(c, w_ada, b_ada):
    depth, d, nd = w_ada.shape
    b = c.shape[0]
    return pl.pallas_call(
        _ada_kernel,
        grid=(depth, nd // d),
        in_specs=[
            pl.BlockSpec((b, d), lambda l, j: (0, 0)),
            pl.BlockSpec((1, d, d), lambda l, j: (l, 0, j)),
            pl.BlockSpec((1, 1, d), lambda l, j: (l, 0, j)),
        ],
        out_specs=pl.BlockSpec((1, b, d), lambda l, j: (l, 0, j)),
        out_shape=jax.ShapeDtypeStruct((depth, b, nd), F32),
        name="ada_mod",
    )(c, w_ada, b_ada.reshape(depth, 1, nd))


def _front_kernel(x_ref, mod_ref, nrm_ref, w_in_ref, wpool_ref, pscale_ref, sgn_ref, wcat_ref,
                  sbias_ref, mnp_ref, mng_ref, q_ref, k_ref, v_ref, pg_ref, e_ref, *, tm, sbw, pw):
    s = pl.program_id(1)
    sub = FRONT_SUB
    c0 = 3 * sbw
    c1 = c0 + pw
    heads_per_group = LANES // HEAD_DIM
    gdim = pw // len(POOL_WINDOWS)
    hd = pw // SG_HEADS
    lane_g = lax.broadcasted_iota(jnp.int32, (sub, LANES), 1)
    lane_e = lax.broadcasted_iota(jnp.int32, (sub + POOL_HALO, pw), 1)
    lane = lax.broadcasted_iota(jnp.int32, (sub, pw), 1)
    row = lax.broadcasted_iota(jnp.int32, (sub, pw), 0)
    lane_c = lax.broadcasted_iota(jnp.int32, (CHUNK, pw), 1)
    wlen = jnp.where(lane < gdim, POOL_WINDOWS[0],
                     jnp.where(lane < 2 * gdim, POOL_WINDOWS[1],
                               jnp.where(lane < 3 * gdim, POOL_WINDOWS[2], POOL_WINDOWS[3])))

    @pl.when(s == 0)
    def _():
        e_ref[0:POOL_HALO, :] = jnp.zeros((POOL_HALO, pw), F32)

    @pl.when(s > 0)
    def _():
        e_ref[0:POOL_HALO, :] = e_ref[tm:tm + POOL_HALO, :]

    def normed(r):
        x = x_ref[0, r * sub:(r + 1) * sub, :]
        h = _rms(x, nrm_ref[...]) * (1.0 + mod_ref[0, 1:2, :]) + mod_ref[0, 0:1, :]
        return h.astype(BF16)

    def project(r, hb):
        rows = slice(r * sub, (r + 1) * sub)
        for t, ref in enumerate((q_ref, k_ref, v_ref)):
            y = jnp.dot(hb, w_in_ref[:, t * sbw:(t + 1) * sbw], preferred_element_type=F32)
            for j in range(sbw // LANES):
                grp = y[:, j * LANES:(j + 1) * LANES]
                for hh in range(heads_per_group):
                    own = (lane_g >= hh * HEAD_DIM) & (lane_g < (hh + 1) * HEAD_DIM)
                    fill = jnp.where(lane_g == _spare_lane(hh), 1.0, 0.0) if t == 0 else 0.0
                    ref[0, heads_per_group * j + hh, rows, :] = jnp.where(own, grp, fill).astype(BF16)
        return jnp.dot(hb, w_in_ref[:, c0:], preferred_element_type=F32)

    def pool(r, p):
        e_ref[POOL_HALO + r * sub:POOL_HALO + (r + 1) * sub, :] = p
        e = e_ref[r * sub:(r + 1) * sub + POOL_HALO, :]
        s2 = e + pltpu.roll(e, 1, 0)
        s4 = s2 + pltpu.roll(s2, 2, 0)
        s8 = s4 + pltpu.roll(s4, 4, 0)
        s16 = s8 + pltpu.roll(s8, 8, 0)
        wsum = jnp.where(lane_e < gdim, s2,
                         jnp.where(lane_e < 2 * gdim, s4, jnp.where(lane_e < 3 * gdim, s8, s16)))
        wsum = wsum[POOL_HALO:, :]
        count = jnp.minimum(s * tm + r * sub + row + 1, wlen).astype(F32)
        pooled = wsum / count - p
        mixed = jnp.dot(pooled.astype(BF16), wpool_ref[...], preferred_element_type=F32) * pscale_ref[...]
        pg_ref[0, r * sub:(r + 1) * sub, 0:pw] = _rms(mixed, mnp_ref[...]).astype(BF16)

    def gate(r, u, vg):
        ug = jax.nn.gelu(u)
        vge = jax.nn.gelu(vg)
        mu = jnp.mean(vge, axis=-1, keepdims=True)
        var = jnp.mean(jnp.square(vge - mu), axis=-1, keepdims=True)
        vn = ((vge - mu) * lax.rsqrt(var + EPS) * sgn_ref[...]).astype(BF16)
        zero = jnp.zeros((CHUNK, pw), BF16)
        outs = []
        for c in range(sub // CHUNK):
            vc = vn[c * CHUNK:(c + 1) * CHUNK, :]
            stacked = jnp.concatenate(
                [jnp.where((lane_c >= hh * hd) & (lane_c < (hh + 1) * hd), vc, zero) for hh in range(SG_HEADS)],
                axis=0)
            outs.append(jnp.dot(wcat_ref[...], stacked, preferred_element_type=F32) + sbias_ref[...])
        g_out = ug * jnp.concatenate(outs, axis=0)
        pg_ref[0, r * sub:(r + 1) * sub, pw:2 * pw] = _rms(g_out, mng_ref[...]).astype(BF16)

    n_sub = tm // sub
    hbs = [normed(r) for r in range(n_sub)]
    mixers = [project(r, hbs[r]) for r in range(n_sub)]
    for r in range(n_sub):
        pool(r, mixers[r][:, :pw])
        gate(r, mixers[r][:, pw:2 * pw], mixers[r][:, 2 * pw:])


def _front(x, mod, nrm, w_in_b, wpool_bd, pscale, sgn, wcat, sbias, mnp, mng, *, layer, tm, sbw, pw):
    b, s, d = x.shape
    nh = sbw // HEAD_DIM
    in_w = w_in_b.shape[2]
    const = lambda *shape: pl.BlockSpec(shape, lambda bi, si: (0,) * len(shape))
    stacked = lambda *shape: pl.BlockSpec((None,) + shape, lambda bi, si: (layer,) + (0,) * len(shape))
    qkv_spec = pl.BlockSpec((1, nh, tm, LANES), lambda bi, si: (bi, 0, si, 0))
    qkv_shape = jax.ShapeDtypeStruct((b, nh, s, LANES), BF16)
    return pl.pallas_call(
        functools.partial(_front_kernel, tm=tm, sbw=sbw, pw=pw),
        grid=(b, s // tm),
        in_specs=[
            pl.BlockSpec((1, tm, d), lambda bi, si: (bi, si, 0)),
            pl.BlockSpec((1, N_MOD, d), lambda bi, si: (bi, 0, 0)),
            const(1, d), stacked(d, in_w), const(pw, pw), const(1, pw), const(1, pw),
            const(CHUNK, SG_HEADS * CHUNK), const(CHUNK, pw), const(1, pw), const(1, pw),
        ],
        out_specs=[qkv_spec, qkv_spec, qkv_spec,
                   pl.BlockSpec((1, tm, 2 * pw), lambda bi, si: (bi, si, 0))],
        out_shape=[qkv_shape, qkv_shape, qkv_shape, jax.ShapeDtypeStruct((b, s, 2 * pw), BF16)],
        scratch_shapes=[pltpu.VMEM((tm + POOL_HALO, pw), F32)],
        compiler_params=pltpu.CompilerParams(
            dimension_semantics=("arbitrary", "arbitrary"), vmem_limit_bytes=VMEM_LIMIT),
        name="front",
    )(x, mod, nrm, w_in_b, wpool_bd, pscale, sgn, wcat, sbias, mnp, mng)


TRI_WIDTH = 256


def _neg_abs(z):
    sign = jnp.uint32(0x80000000)
    return lax.bitcast_convert_type(lax.bitcast_convert_type(z, jnp.uint32) | sign, F32)


def _mixer_ffn_kernel(q_ref, kp_ref, kc_ref, vp_ref, vc_ref, pg_ref, x_ref, moda_ref, modf_ref, lc_ref, tri_ref,
                      mna_ref, wout_ref, nrm_ref, wg_ref, wu_ref, wd_ref, fn_ref, o_ref, x1_scr, res_scr,
                      *, tq, nh, n_seq, n_tiles, final):
    j = pl.program_id(0)
    i = jnp.minimum(j, n_tiles - 1) % n_seq
    nqb = tq // Q_BLOCK
    edge = KV_SPAN - Q_BLOCK
    group = LANES // HEAD_DIM
    assert KV_SPAN - 2 * Q_BLOCK + Q_BLOCK == 2 * TRI_WIDTH

    @pl.when(j == 0)
    def _():
        x1_scr[...] = jnp.zeros(x1_scr.shape, F32)

    res_scr[...] = x1_scr[...]

    sub = FFN_SUB
    subs = [slice(r * sub, (r + 1) * sub) for r in range(tq // sub)]
    dff = wg_ref.shape[1]
    d_model = wd_ref.shape[1]
    n_ct = dff // FFN_COLS
    hbs, acts = {}, {}

    def ffn_norm(r):
        h = _rms(res_scr[subs[r], :], nrm_ref[...]) * (1.0 + modf_ref[0, 4:5, :]) + modf_ref[0, 3:4, :]
        hbs[r] = h.astype(BF16)

    def ffn_gate_up(r, c):
        cols = slice(c * FFN_COLS, (c + 1) * FFN_COLS)
        gate = jnp.dot(hbs[r], wg_ref[:, cols], preferred_element_type=F32)
        up = jnp.dot(hbs[r], wu_ref[:, cols], preferred_element_type=F32)
        acts[r, c] = (gate / (1.0 + jnp.exp(-gate)) * up).astype(BF16)

    def ffn_down(r, c):
        cols = slice(c * FFN_COLS, (c + 1) * FFN_COLS)
        act = jnp.concatenate([acts[r, k] for k in range(n_ct)], axis=1)
        y = jnp.dot(act, wd_ref[:, cols], preferred_element_type=F32)
        out = res_scr[subs[r], cols] + modf_ref[0, 5:6, cols] * y
        o_ref[0, subs[r], cols] = out

    def ffn_finish(r):
        if final:
            o_ref[0, subs[r], :] = _rms(o_ref[0, subs[r], :], fn_ref[...])

    pieces = []
    for r in range(len(subs)):
        pieces += [functools.partial(ffn_gate_up, r, c) for c in range(n_ct)]
        pieces += [functools.partial(ffn_down, r, c) for c in range(d_model // FFN_COLS)]
        pieces.append(functools.partial(ffn_finish, r))

    kb_prev = jnp.where(i == 0, NEG_BIG, 0.0).astype(BF16)
    prev_bias = [kb_prev * lc_ref[hh:hh + 1, :] for hh in range(group)]

    def window(prev_ref, cur_ref, h, off):
        return prev_ref[0, h, off:, :], cur_ref[0, h, 0:off + Q_BLOCK, :]

    newest = (lax.broadcasted_iota(jnp.int32, (Q_BLOCK, Q_BLOCK), 1)
              < lax.broadcasted_iota(jnp.int32, (Q_BLOCK, Q_BLOCK), 0))
    half = TRI_WIDTH - Q_BLOCK

    def scores(item):
        qb, h = divmod(item, nh)
        off = qb * Q_BLOCK
        k_prev, k_cur = window(kp_ref, kc_ref, h, off)
        keys = jnp.concatenate([k_prev + prev_bias[h % group], k_cur], axis=0)
        z = lax.dot_general(q_ref[0, h, off:off + Q_BLOCK, :], keys, (((1,), (1,)), ((), ())),
                            preferred_element_type=F32)
        return jnp.concatenate([jnp.where(newest, z[:, edge:], z[:, :Q_BLOCK]), z[:, Q_BLOCK:edge]], axis=1)

    def softplus(z):
        n = jnp.maximum(z, 0.0) + jnp.log(1.0 + jnp.exp2(_neg_abs(z))) * LOG2E
        nb = n.astype(BF16)
        n_new = jnp.where(newest, n[:, :Q_BLOCK], 0.0)
        nb_new = n_new.astype(BF16)
        nb_old = nb[:, :Q_BLOCK] - nb_new
        tot_new = jnp.sum(n_new, axis=-1, keepdims=True)
        tot_mid = jnp.sum(n[:, Q_BLOCK + half:], axis=-1, keepdims=True)
        chunks = (jnp.concatenate([nb_old, nb[:, Q_BLOCK:Q_BLOCK + half]], axis=1), nb[:, Q_BLOCK + half:], nb_new)
        return z - n, chunks, (tot_mid + tot_new, tot_new)

    def log_weight(state):
        dmn, (c_first, c_mid, c_new), (tot_first, tot_mid) = state
        l_first = jnp.dot(c_first, tri_ref[...], preferred_element_type=F32) + tot_first
        l_mid = jnp.dot(c_mid, tri_ref[...], preferred_element_type=F32) + tot_mid
        l_new = jnp.dot(c_new, tri_ref[0:Q_BLOCK, 0:Q_BLOCK], preferred_element_type=F32)
        later = jnp.concatenate([jnp.where(newest, l_new, l_first[:, :Q_BLOCK]), l_first[:, Q_BLOCK:], l_mid],
                                axis=1)
        return dmn - later

    def weights(arg):
        w = jnp.exp2(arg)
        wb = w.astype(BF16)
        w_new = jnp.where(newest, w[:, :Q_BLOCK], 0.0).astype(BF16)
        return jnp.concatenate([wb[:, :Q_BLOCK] - w_new, wb[:, Q_BLOCK:], w_new], axis=1)

    def weighted_values(item, w):
        qb, h = divmod(item, nh)
        values = jnp.concatenate(window(vp_ref, vc_ref, h, qb * Q_BLOCK), axis=0)
        return jnp.dot(w, values, preferred_element_type=F32)

    def project(qb, pvs):
        a = jnp.concatenate([sum(pvs[g * group + 1:(g + 1) * group], pvs[g * group])
                             for g in range(nh // group)], axis=1)
        rows = slice(qb * Q_BLOCK, (qb + 1) * Q_BLOCK)
        an = _rms(a, mna_ref[...]).astype(BF16)
        merged = jnp.concatenate([an, pg_ref[0, rows, :]], axis=1)
        y = jnp.dot(merged, wout_ref[...], preferred_element_type=F32)
        x1_scr[rows, :] = x_ref[0, rows, :] + moda_ref[0, 2:3, :] * y

    n_items = nqb * nh
    per_step = -(-len(pieces) // n_items)
    zs, states, ws, pvs = {}, {}, {}, {}
    for step in range(-2, n_items + 2):
        if step == 0:
            for r in range(len(subs)):
                ffn_norm(r)
        for _ in range(per_step):
            if pieces and step >= 0:
                pieces.pop(0)()
        if 0 <= step + 2 < n_items:
            zs[step + 2] = scores(step + 2)
        if 0 <= step < n_items:
            states[step] = softplus(zs.pop(step))
        if 0 <= step - 1 < n_items:
            ws[step - 1] = weights(log_weight(states.pop(step - 1)))
        if 0 <= step - 2 < n_items:
            item = step - 2
            pvs[item] = weighted_values(item, ws.pop(item))
            if item % nh == nh - 1:
                qb = item // nh
                project(qb, [pvs.pop(qb * nh + h) for h in range(nh)])
    assert not pieces


def _mixer_ffn(q, k, v, pg, x, mod, lane_consts, tri, mna, wout_b, nrm, wgu_b, wd_b, fn, *, layer, tq, final):
    b, s, d = x.shape
    nh = q.shape[1]
    mixw = wout_b.shape[1]
    dff = wd_b.shape[1]
    n_seq = s // tq
    n_tiles = b * n_seq
    once = lambda *shape: pl.BlockSpec(shape, lambda j: (0,) * len(shape), pipeline_mode=pl.Buffered(1))
    stacked = lambda *shape, col=0: pl.BlockSpec((None,) + shape, lambda j: (layer, 0, col),
                                                 pipeline_mode=pl.Buffered(1))
    mixer_tile = lambda j: jnp.minimum(j, n_tiles - 1)
    ffn_tile = lambda j: jnp.maximum(j - 1, 0)
    cur = pl.BlockSpec((1, nh, tq, LANES), lambda j: (mixer_tile(j) // n_seq, 0, mixer_tile(j) % n_seq, 0))
    prev = pl.BlockSpec((1, nh, tq, LANES),
                        lambda j: (mixer_tile(j) // n_seq, 0, jnp.maximum(mixer_tile(j) % n_seq - 1, 0), 0))
    rows_a = lambda width: pl.BlockSpec((1, tq, width),
                                        lambda j: (mixer_tile(j) // n_seq, mixer_tile(j) % n_seq, 0))
    return pl.pallas_call(
        functools.partial(_mixer_ffn_kernel, tq=tq, nh=nh, n_seq=n_seq, n_tiles=n_tiles, final=final),
        grid=(n_tiles + 1,),
        in_specs=[
            cur, prev, cur, prev, cur, rows_a(pg.shape[2]), rows_a(d),
            pl.BlockSpec((1, N_MOD, d), lambda j: (mixer_tile(j) // n_seq, 0, 0)),
            pl.BlockSpec((1, N_MOD, d), lambda j: (ffn_tile(j) // n_seq, 0, 0)),
            once(*lane_consts.shape), once(*tri.shape), once(1, nh * HEAD_DIM), stacked(mixw, d),
            once(1, d), stacked(d, dff, col=0), stacked(d, dff, col=1), stacked(dff, d), once(1, d),
        ],
        out_specs=pl.BlockSpec((1, tq, d), lambda j: (ffn_tile(j) // n_seq, ffn_tile(j) % n_seq, 0)),
        out_shape=jax.ShapeDtypeStruct((b, s, d), F32),
        scratch_shapes=[pltpu.VMEM((tq, d), F32), pltpu.VMEM((tq, d), F32)],
        compiler_params=pltpu.CompilerParams(dimension_semantics=("arbitrary",), vmem_limit_bytes=VMEM_LIMIT),
        name="mixer_ffn",
    )(q, k, k, v, v, pg, x, mod, mod, lane_consts, tri, mna, wout_b, nrm, wgu_b, wgu_b, wd_b, fn)


def _lane_constants():
    rows = np.zeros((16, LANES), np.float32)
    for hh in range(LANES // HEAD_DIM):
        rows[hh, _spare_lane(hh)] = 1.0
    return rows


def _suffix_sum_matrix(n):
    j = np.arange(n)
    return (j[:, None] > j[None, :]).astype(np.float32)


def kernel(x, c, w_ada, b_ada, norm_mix_in, w_in, w_pool, pool_scale, sg_norm, w_s, b_s, mix_norm, w_out,
           norm_ffn_in, w_gate_up, w_down, final_norm):
    b, s, d = x.shape
    depth = w_ada.shape[0]
    pw = pool_scale.shape[1]
    sbw = (w_in.shape[2] - 3 * pw) // 3
    dff = w_down.shape[1]
    tm = min(ROW_TILE, s)
    tf = min(FRONT_TILE, s)
    assert s % tm == 0 and tm % Q_BLOCK == 0 and tm >= SB_WINDOW and sbw % LANES == 0
    assert s % tf == 0 and tf % FRONT_SUB == 0 and FRONT_SUB % CHUNK == 0
    assert tm % FFN_SUB == 0

    mod_all = _ada(c, w_ada, b_ada).reshape(depth, b, N_MOD, d)

    q_scale = jnp.concatenate([jnp.full((sbw,), LOG2E * HEAD_DIM ** -0.5, F32),
                               jnp.ones((w_in.shape[2] - sbw,), F32)])
    w_in_b = (w_in * q_scale).astype(BF16)
    groups = w_pool.shape[1]
    gdim = w_pool.shape[2]
    eye = jnp.eye(groups, dtype=F32)
    wpool_bd = (w_pool[:, :, :, None, :] * eye[None, :, None, :, None]).reshape(
        depth, groups * gdim, groups * gdim).astype(BF16)
    causal = jnp.tril(jnp.ones((CHUNK, CHUNK), dtype=bool))
    ws_c = jnp.where(causal[None, None], w_s, 0.0)
    wcat = ws_c.transpose(0, 2, 1, 3).reshape(depth, CHUNK, SG_HEADS * CHUNK).astype(BF16)
    sbias = jnp.repeat(b_s.transpose(0, 2, 1), pw // SG_HEADS, axis=2)
    w_out_b = w_out.astype(BF16)
    wgu_b = w_gate_up.astype(BF16)
    wd_b = w_down.astype(BF16)
    lane_consts = jnp.asarray(_lane_constants(), dtype=BF16)
    tri = jnp.asarray(_suffix_sum_matrix(TRI_WIDTH), dtype=BF16)
    row = lambda v: v.reshape(1, -1)

    for l in range(depth):
        mod = mod_all[l]
        q, k, v, pg = _front(x, mod, row(norm_mix_in[l]), w_in_b, wpool_bd[l], row(pool_scale[l]),
                             row(sg_norm[l]), wcat[l], sbias[l], row(mix_norm[l, sbw:sbw + pw]),
                             row(mix_norm[l, sbw + pw:]), layer=l, tm=tf, sbw=sbw, pw=pw)
        x = _mixer_ffn(q, k, v, pg, x, mod, lane_consts, tri, row(mix_norm[l, :sbw]), w_out_b,
                       row(norm_ffn_in[l]), wgu_b, wd_b, row(final_norm),
                       layer=l, tq=tm, final=(l == depth - 1))
    return x
```

```python
import functools

import numpy as np
import jax
import jax.numpy as jnp
from jax import lax
from jax.experimental import pallas as pl
from jax.experimental.pallas import tpu as pltpu

F32 = jnp.float32
BF16 = jnp.bfloat16

HEAD_DIM = 64
Q_BLOCK = 128
SB_WINDOW = 512
KV_SPAN = SB_WINDOW + Q_BLOCK
POOL_WINDOWS = (2, 4, 8, 16)
POOL_HALO = 16
CHUNK = 128
SG_HEADS = 4
N_MOD = 6
EPS = 1e-6
LANES = 128
NEG_BIG = -1e30
LOG2E = 1.4426950408889634

ROW_TILE = 512
FRONT_TILE = 1024
FRONT_SUB = 256
FFN_SUB = 256
FFN_COLS = 256
VMEM_LIMIT = 56 * 1024 * 1024


def _spare_lane(head_in_group):
    return HEAD_DIM if head_in_group == 0 else 0


def _rms(x, gain):
    return x * lax.rsqrt(jnp.mean(x * x, axis=-1, keepdims=True) + EPS) * gain


def _ada_kernel(c_ref, w_ref, b_ref, o_ref):
    c = c_ref[...]
    c_act = c / (1.0 + jnp.exp(-c))
    o_ref[0] = jnp.dot(c_act, w_ref[0], preferred_element_type=F32,
                       precision=lax.Precision.HIGHEST) + b_ref[0]


def _ada(c, w_ada, b_ada):
    depth, d, nd = w_ada.shape
    b = c.shape[0]
    return pl.pallas_call(
        _ada_kernel,
        grid=(depth, nd // d),
        in_specs=[
            pl.BlockSpec((b, d), lambda l, j: (0, 0)),
            pl.BlockSpec((1, d, d), lambda l, j: (l, 0, j)),
            pl.BlockSpec((1, 1, d), lambda l, j: (l, 0, j)),
        ],
        out_specs=pl.BlockSpec((1, b, d), lambda l, j: (l, 0, j)),
        out_shape=jax.ShapeDtypeStruct((depth, b, nd), F32),
        name="ada_mod",
    )(c, w_ada, b_ada.reshape(depth, 1, nd))


def _front_kernel(x_ref, mod_ref, nrm_ref, w_in_ref, wpool_ref, pscale_ref, sgn_ref, wcat_ref,
                  sbias_ref, mnp_ref, mng_ref, q_ref, k_ref, v_ref, pg_ref, e_ref, *, tm, sbw, pw):
    s = pl.program_id(1)
    sub = FRONT_SUB
    c0 = 3 * sbw
    c1 = c0 + pw
    heads_per_group = LANES // HEAD_DIM
    gdim = pw // len(POOL_WINDOWS)
    hd = pw // SG_HEADS
    lane_g = lax.broadcasted_iota(jnp.int32, (sub, LANES), 1)
    lane_e = lax.broadcasted_iota(jnp.int32, (sub + POOL_HALO, pw), 1)
    lane = lax.broadcasted_iota(jnp.int32, (sub, pw), 1)
    row = lax.broadcasted_iota(jnp.int32, (sub, pw), 0)
    lane_c = lax.broadcasted_iota(jnp.int32, (CHUNK, pw), 1)
    wlen = jnp.where(lane < gdim, POOL_WINDOWS[0],
                     jnp.where(lane < 2 * gdim, POOL_WINDOWS[1],
                               jnp.where(lane < 3 * gdim, POOL_WINDOWS[2], POOL_WINDOWS[3])))

    @pl.when(s == 0)
    def _():
        e_ref[0:POOL_HALO, :] = jnp.zeros((POOL_HALO, pw), F32)

    @pl.when(s > 0)
    def _():
        e_ref[0:POOL_HALO, :] = e_ref[tm:tm + POOL_HALO, :]

    def normed(r):
        x = x_ref[0, r * sub:(r + 1) * sub, :]
        h = _rms(x, nrm_ref[...]) * (1.0 + mod_ref[0, 1:2, :]) + mod_ref[0, 0:1, :]
        return h.astype(BF16)

    def project(r, hb):
        rows = slice(r * sub, (r + 1) * sub)
        mixer_in = jnp.dot(hb, w_in_ref[:, c0:], preferred_element_type=F32)
        for t, ref in enumerate((q_ref, k_ref, v_ref)):
            y = jnp.dot(hb, w_in_ref[:, t * sbw:(t + 1) * sbw], preferred_element_type=F32)
            for j in range(sbw // LANES):
                grp = y[:, j * LANES:(j + 1) * LANES]
                for hh in range(heads_per_group):
                    own = (lane_g >= hh * HEAD_DIM) & (lane_g < (hh + 1) * HEAD_DIM)
                    fill = jnp.where(lane_g == _spare_lane(hh), 1.0, 0.0) if t == 0 else 0.0
                    ref[0, heads_per_group * j + hh, rows, :] = jnp.where(own, grp, fill).astype(BF16)
        return mixer_in

    def pool(r, p):
        e_ref[POOL_HALO + r * sub:POOL_HALO + (r + 1) * sub, :] = p
        e = e_ref[r * sub:(r + 1) * sub + POOL_HALO, :]
        s2 = e + pltpu.roll(e, 1, 0)
        s4 = s2 + pltpu.roll(s2, 2, 0)
        s8 = s4 + pltpu.roll(s4, 4, 0)
        s16 = s8 + pltpu.roll(s8, 8, 0)
        wsum = jnp.where(lane_e < gdim, s2,
                         jnp.where(lane_e < 2 * gdim, s4, jnp.where(lane_e < 3 * gdim, s8, s16)))
        wsum = wsum[POOL_HALO:, :]
        count = jnp.minimum(s * tm + r * sub + row + 1, wlen).astype(F32)
        pooled = wsum / count - p
        mixed = jnp.dot(pooled.astype(BF16), wpool_ref[...], preferred_element_type=F32) * pscale_ref[...]
        pg_ref[0, r * sub:(r + 1) * sub, 0:pw] = _rms(mixed, mnp_ref[...]).astype(BF16)

    def gate(r, u, vg):
        ug = jax.nn.gelu(u)
        vge = jax.nn.gelu(vg)
        mu = jnp.mean(vge, axis=-1, keepdims=True)
        var = jnp.mean(jnp.square(vge - mu), axis=-1, keepdims=True)
        vn = ((vge - mu) * lax.rsqrt(var + EPS) * sgn_ref[...]).astype(BF16)
        zero = jnp.zeros((CHUNK, pw), BF16)
        outs = []
        for c in range(sub // CHUNK):
            vc = vn[c * CHUNK:(c + 1) * CHUNK, :]
            stacked = jnp.concatenate(
                [jnp.where((lane_c >= hh * hd) & (lane_c < (hh + 1) * hd), vc, zero) for hh in range(SG_HEADS)],
                axis=0)
            outs.append(jnp.dot(wcat_ref[...], stacked, preferred_element_type=F32) + sbias_ref[...])
        g_out = ug * jnp.concatenate(outs, axis=0)
        pg_ref[0, r * sub:(r + 1) * sub, pw:2 * pw] = _rms(g_out, mng_ref[...]).astype(BF16)

    n_sub = tm // sub
    hbs = [normed(r) for r in range(n_sub)]
    for r in range(n_sub):
        mixer_in = project(r, hbs[r])
        pool(r, mixer_in[:, :pw])
        gate(r, mixer_in[:, pw:2 * pw], mixer_in[:, 2 * pw:])


def _front(x, mod, nrm, w_in_b, wpool_bd, pscale, sgn, wcat, sbias, mnp, mng, *, layer, tm, sbw, pw):
    b, s, d = x.shape
    nh = sbw // HEAD_DIM
    in_w = w_in_b.shape[2]
    const = lambda *shape: pl.BlockSpec(shape, lambda bi, si: (0,) * len(shape))
    stacked = lambda *shape: pl.BlockSpec((None,) + shape, lambda bi, si: (layer,) + (0,) * len(shape))
    qkv_spec = pl.BlockSpec((1, nh, tm, LANES), lambda bi, si: (bi, 0, si, 0))
    qkv_shape = jax.ShapeDtypeStruct((b, nh, s, LANES), BF16)
    return pl.pallas_call(
        functools.partial(_front_kernel, tm=tm, sbw=sbw, pw=pw),
        grid=(b, s // tm),
        in_specs=[
            pl.BlockSpec((1, tm, d), lambda bi, si: (bi, si, 0)),
            pl.BlockSpec((1, N_MOD, d), lambda bi, si: (bi, 0, 0)),
            const(1, d), stacked(d, in_w), const(pw, pw), const(1, pw), const(1, pw),
            const(CHUNK, SG_HEADS * CHUNK), const(CHUNK, pw), const(1, pw), const(1, pw),
        ],
        out_specs=[qkv_spec, qkv_spec, qkv_spec,
                   pl.BlockSpec((1, tm, 2 * pw), lambda bi, si: (bi, si, 0))],
        out_shape=[qkv_shape, qkv_shape, qkv_shape, jax.ShapeDtypeStruct((b, s, 2 * pw), BF16)],
        scratch_shapes=[pltpu.VMEM((tm + POOL_HALO, pw), F32)],
        compiler_params=pltpu.CompilerParams(
            dimension_semantics=("arbitrary", "arbitrary"), vmem_limit_bytes=VMEM_LIMIT),
        name="front",
    )(x, mod, nrm, w_in_b, wpool_bd, pscale, sgn, wcat, sbias, mnp, mng)


TRI_WIDTH = 256


def _neg_abs(z):
    sign = jnp.uint32(0x80000000)
    return lax.bitcast_convert_type(lax.bitcast_convert_type(z, jnp.uint32) | sign, F32)


def _mixer_ffn_kernel(q_ref, kp_ref, kc_ref, vp_ref, vc_ref, pg_ref, x_ref, moda_ref, modf_ref, lc_ref, tri_ref,
                      mna_ref, wout_ref, nrm_ref, wg_ref, wu_ref, wd_ref, fn_ref, o_ref, x1_scr, res_scr,
                      *, tq, nh, n_seq, n_tiles, final):
    j = pl.program_id(0)
    i = jnp.minimum(j, n_tiles - 1) % n_seq
    nqb = tq // Q_BLOCK
    edge = KV_SPAN - Q_BLOCK
    group = LANES // HEAD_DIM
    assert KV_SPAN - 2 * Q_BLOCK + Q_BLOCK == 2 * TRI_WIDTH

    @pl.when(j == 0)
    def _():
        x1_scr[...] = jnp.zeros(x1_scr.shape, F32)

    res_scr[...] = x1_scr[...]

    sub = FFN_SUB
    subs = [slice(r * sub, (r + 1) * sub) for r in range(tq // sub)]
    dff = wg_ref.shape[1]
    d_model = wd_ref.shape[1]
    n_ct = dff // FFN_COLS
    hbs, acts = {}, {}

    def ffn_norm(r):
        h = _rms(res_scr[subs[r], :], nrm_ref[...]) * (1.0 + modf_ref[0, 4:5, :]) + modf_ref[0, 3:4, :]
        hbs[r] = h.astype(BF16)

    def ffn_gate_up(r, c):
        cols = slice(c * FFN_COLS, (c + 1) * FFN_COLS)
        gate = jnp.dot(hbs[r], wg_ref[:, cols], preferred_element_type=F32)
        up = jnp.dot(hbs[r], wu_ref[:, cols], preferred_element_type=F32)
        acts[r, c] = (gate / (1.0 + jnp.exp(-gate)) * up).astype(BF16)

    def ffn_down(r, c):
        cols = slice(c * FFN_COLS, (c + 1) * FFN_COLS)
        act = jnp.concatenate([acts[r, k] for k in range(n_ct)], axis=1)
        y = jnp.dot(act, wd_ref[:, cols], preferred_element_type=F32)
        out = res_scr[subs[r], cols] + modf_ref[0, 5:6, cols] * y
        o_ref[0, subs[r], cols] = out

    def ffn_finish(r):
        if final:
            o_ref[0, subs[r], :] = _rms(o_ref[0, subs[r], :], fn_ref[...])

    pieces = []
    for r in range(len(subs)):
        pieces += [functools.partial(ffn_gate_up, r, c) for c in range(n_ct)]
        pieces += [functools.partial(ffn_down, r, c) for c in range(d_model // FFN_COLS)]
        pieces.append(functools.partial(ffn_finish, r))

    kb_prev = jnp.where(i == 0, NEG_BIG, 0.0).astype(BF16)
    prev_bias = [kb_prev * lc_ref[hh:hh + 1, :] for hh in range(group)]

    def window(prev_ref, cur_ref, h, off):
        return prev_ref[0, h, off:, :], cur_ref[0, h, 0:off + Q_BLOCK, :]

    newest = (lax.broadcasted_iota(jnp.int32, (Q_BLOCK, Q_BLOCK), 1)
              < lax.broadcasted_iota(jnp.int32, (Q_BLOCK, Q_BLOCK), 0))
    half = TRI_WIDTH - Q_BLOCK

    def scores(item):
        qb, h = divmod(item, nh)
        off = qb * Q_BLOCK
        k_prev, k_cur = window(kp_ref, kc_ref, h, off)
        keys = jnp.concatenate([k_prev + prev_bias[h % group], k_cur], axis=0)
        z = lax.dot_general(q_ref[0, h, off:off + Q_BLOCK, :], keys, (((1,), (1,)), ((), ())),
                            preferred_element_type=F32)
        return jnp.concatenate([jnp.where(newest, z[:, edge:], z[:, :Q_BLOCK]), z[:, Q_BLOCK:edge]], axis=1)

    def softplus(z):
        n = jnp.maximum(z, 0.0) + jnp.log(1.0 + jnp.exp2(_neg_abs(z))) * LOG2E
        nb = n.astype(BF16)
        n_new = jnp.where(newest, n[:, :Q_BLOCK], 0.0)
        nb_new = n_new.astype(BF16)
        nb_old = nb[:, :Q_BLOCK] - nb_new
        tot_new = jnp.sum(n_new, axis=-1, keepdims=True)
        tot_mid = jnp.sum(n[:, Q_BLOCK + half:], axis=-1, keepdims=True)
        chunks = (jnp.concatenate([nb_old, nb[:, Q_BLOCK:Q_BLOCK + half]], axis=1), nb[:, Q_BLOCK + half:], nb_new)
        return z - n, chunks, (tot_mid + tot_new, tot_new)

    def log_weight(state):
        dmn, (c_first, c_mid, c_new), (tot_first, tot_mid) = state
        l_first = jnp.dot(c_first, tri_ref[...], preferred_element_type=F32) + tot_first
        l_mid = jnp.dot(c_mid, tri_ref[...], preferred_element_type=F32) + tot_mid
        l_new = jnp.dot(c_new, tri_ref[0:Q_BLOCK, 0:Q_BLOCK], preferred_element_type=F32)
        later = jnp.concatenate([jnp.where(newest, l_new, l_first[:, :Q_BLOCK]), l_first[:, Q_BLOCK:], l_mid],
                                axis=1)
        return dmn - later

    def weights(arg):
        w = jnp.exp2(arg)
        wb = w.astype(BF16)
        w_new = jnp.where(newest, w[:, :Q_BLOCK], 0.0).astype(BF16)
        return jnp.concatenate([wb[:, :Q_BLOCK] - w_new, wb[:, Q_BLOCK:], w_new], axis=1)

    def weighted_values(item, w):
        qb, h = divmod(item, nh)
        values = jnp.concatenate(window(vp_ref, vc_ref, h, qb * Q_BLOCK), axis=0)
        return jnp.dot(w, values, preferred_element_type=F32)

    def project(qb, pvs):
        a = jnp.concatenate([sum(pvs[g * group + 1:(g + 1) * group], pvs[g * group])
                             for g in range(nh // group)], axis=1)
        rows = slice(qb * Q_BLOCK, (qb + 1) * Q_BLOCK)
        an = _rms(a, mna_ref[...]).astype(BF16)
        merged = jnp.concatenate([an, pg_ref[0, rows, :]], axis=1)
        y = jnp.dot(merged, wout_ref[...], preferred_element_type=F32)
        x1_scr[rows, :] = x_ref[0, rows, :] + moda_ref[0, 2:3, :] * y

    n_items = nqb * nh
    per_step = -(-len(pieces) // n_items)
    zs, ws, pvs = {}, {}, {}
    for step in range(-2, n_items + 1):
        if step == 0:
            for r in range(len(subs)):
                ffn_norm(r)
        for _ in range(per_step):
            if pieces and step >= 0:
                pieces.pop(0)()
        if 0 <= step + 2 < n_items:
            zs[step + 2] = scores(step + 2)
        if 0 <= step < n_items:
            ws[step] = weights(log_weight(softplus(zs.pop(step))))
        if 0 <= step - 1 < n_items:
            item = step - 1
            pvs[item] = weighted_values(item, ws.pop(item))
            if item % nh == nh - 1:
                qb = item // nh
                project(qb, [pvs.pop(qb * nh + h) for h in range(nh)])
    assert not pieces


def _mixer_ffn(q, k, v, pg, x, mod, lane_consts, tri, mna, wout_b, nrm, wgu_b, wd_b, fn, *, layer, tq, final):
    b, s, d = x.shape
    nh = q.shape[1]
    mixw = wout_b.shape[1]
    dff = wd_b.shape[1]
    n_seq = s // tq
    n_tiles = b * n_seq
    once = lambda *shape: pl.BlockSpec(shape, lambda j: (0,) * len(shape), pipeline_mode=pl.Buffered(1))
    stacked = lambda *shape, col=0: pl.BlockSpec((None,) + shape, lambda j: (layer, 0, col),
                                                 pipeline_mode=pl.Buffered(1))
    mixer_tile = lambda j: jnp.minimum(j, n_tiles - 1)
    ffn_tile = lambda j: jnp.maximum(j - 1, 0)
    cur = pl.BlockSpec((1, nh, tq, LANES), lambda j: (mixer_tile(j) // n_seq, 0, mixer_tile(j) % n_seq, 0))
    prev = pl.BlockSpec((1, nh, tq, LANES),
                        lambda j: (mixer_tile(j) // n_seq, 0, jnp.maximum(mixer_tile(j) % n_seq - 1, 0), 0))
    rows_a = lambda width: pl.BlockSpec((1, tq, width),
                                        lambda j: (mixer_tile(j) // n_seq, mixer_tile(j) % n_seq, 0))
    return pl.pallas_call(
        functools.partial(_mixer_ffn_kernel, tq=tq, nh=nh, n_seq=n_seq, n_tiles=n_tiles, final=final),
        grid=(n_tiles + 1,),
        in_specs=[
            cur, prev, cur, prev, cur, rows_a(pg.shape[2]), rows_a(d),
            pl.BlockSpec((1, N_MOD, d), lambda j: (mixer_tile(j) // n_seq, 0, 0)),
            pl.BlockSpec((1, N_MOD, d), lambda j: (ffn_tile(j) // n_seq, 0, 0)),
            once(*lane_consts.shape), once(*tri.shape), once(1, nh * HEAD_DIM), stacked(mixw, d),
            once(1, d), stacked(d, dff, col=0), stacked(d, dff, col=1), stacked(dff, d), once(1, d),
        ],
        out_specs=pl.BlockSpec((1, tq, d), lambda j: (ffn_tile(j) // n_seq, ffn_tile(j) % n_seq, 0)),
        out_shape=jax.ShapeDtypeStruct((b, s, d), F32),
        scratch_shapes=[pltpu.VMEM((tq, d), F32), pltpu.VMEM((tq, d), F32)],
        compiler_params=pltpu.CompilerParams(dimension_semantics=("arbitrary",), vmem_limit_bytes=VMEM_LIMIT),
        name="mixer_ffn",
    )(q, k, k, v, v, pg, x, mod, mod, lane_consts, tri, mna, wout_b, nrm, wgu_b, wgu_b, wd_b, fn)


def _lane_constants():
    rows = np.zeros((16, LANES), np.float32)
    for hh in range(LANES // HEAD_DIM):
        rows[hh, _spare_lane(hh)] = 1.0
    return rows


def _suffix_sum_matrix(n):
    j = np.arange(n)
    return (j[:, None] > j[None, :]).astype(np.float32)


def kernel(x, c, w_ada, b_ada, norm_mix_in, w_in, w_pool, pool_scale, sg_norm, w_s, b_s, mix_norm, w_out,
           norm_ffn_in, w_gate_up, w_down, final_norm):
    b, s, d = x.shape
    depth = w_ada.shape[0]
    pw = pool_scale.shape[1]
    sbw = (w_in.shape[2] - 3 * pw) // 3
    dff = w_down.shape[1]
    tm = min(ROW_TILE, s)
    tf = min(FRONT_TILE, s)
    assert s % tm == 0 and tm % Q_BLOCK == 0 and tm >= SB_WINDOW and sbw % LANES == 0
    assert s % tf == 0 and tf % FRONT_SUB == 0 and FRONT_SUB % CHUNK == 0
    assert tm % FFN_SUB == 0

    mod_all = _ada(c, w_ada, b_ada).reshape(depth, b, N_MOD, d)

    q_scale = jnp.concatenate([jnp.full((sbw,), LOG2E * HEAD_DIM ** -0.5, F32),
                               jnp.ones((w_in.shape[2] - sbw,), F32)])
    w_in_b = (w_in * q_scale).astype(BF16)
    groups = w_pool.shape[1]
    gdim = w_pool.shape[2]
    eye = jnp.eye(groups, dtype=F32)
    wpool_bd = (w_pool[:, :, :, None, :] * eye[None, :, None, :, None]).reshape(
        depth, groups * gdim, groups * gdim).astype(BF16)
    causal = jnp.tril(jnp.ones((CHUNK, CHUNK), dtype=bool))
    ws_c = jnp.where(causal[None, None], w_s, 0.0)
    wcat = ws_c.transpose(0, 2, 1, 3).reshape(depth, CHUNK, SG_HEADS * CHUNK).astype(BF16)
    sbias = jnp.repeat(b_s.transpose(0, 2, 1), pw // SG_HEADS, axis=2)
    w_out_b = w_out.astype(BF16)
    wgu_b = w_gate_up.astype(BF16)
    wd_b = w_down.astype(BF16)
    lane_consts = jnp.asarray(_lane_constants(), dtype=BF16)
    tri = jnp.asarray(_suffix_sum_matrix(TRI_WIDTH), dtype=BF16)
    row = lambda v: v.reshape(1, -1)

    for l in range(depth):
        mod = mod_all[l]
        q, k, v, pg = _front(x, mod, row(norm_mix_in[l]), w_in_b, wpool_bd[l], row(pool_scale[l]),
                             row(sg_norm[l]), wcat[l], sbias[l], row(mix_norm[l, sbw:sbw + pw]),
                             row(mix_norm[l, sbw + pw:]), layer=l, tm=tf, sbw=sbw, pw=pw)
        x = _mixer_ffn(q, k, v, pg, x, mod, lane_consts, tri, row(mix_norm[l, :sbw]), w_out_b,
                       row(norm_ffn_in[l]), wgu_b, wd_b, row(final_norm),
                       layer=l, tq=tm, final=(l == depth - 1))
    return x
```

```python
import functools

import numpy as np
import jax
import jax.numpy as jnp
from jax import lax
from jax.experimental import pallas as pl
from jax.experimental.pallas import tpu as pltpu

F32 = jnp.float32
BF16 = jnp.bfloat16

HEAD_DIM = 64
Q_BLOCK = 128
SB_WINDOW = 512
KV_SPAN = SB_WINDOW + Q_BLOCK
POOL_WINDOWS = (2, 4, 8, 16)
POOL_HALO = 16
CHUNK = 128
SG_HEADS = 4
N_MOD = 6
EPS = 1e-6
LANES = 128
NEG_BIG = -1e30
LOG2E = 1.4426950408889634

ROW_TILE = 512
FRONT_TILE = 1024
FRONT_SUB = 256
FFN_SUB = 256
FFN_COLS = 256
VMEM_LIMIT = 56 * 1024 * 1024


def _spare_lane(head_in_group):
    return HEAD_DIM if head_in_group == 0 else 0


def _rms(x, gain):
    return x * lax.rsqrt(jnp.mean(x * x, axis=-1, keepdims=True) + EPS) * gain


def _ada_kernel(c_ref, w_ref, b_ref, o_ref):
    c = c_ref[...]
    c_act = c / (1.0 + jnp.exp(-c))
    o_ref[0] = jnp.dot(c_act, w_ref[0], preferred_element_type=F32,
                       precision=lax.Precision.HIGHEST) + b_ref[0]


def _ada(c, w_ada, b_ada):
    depth, d, nd = w_ada.shape
    b = c.shape[0]
    return pl.pallas_call(
        _ada_kernel,
        grid=(depth, nd // d),
        in_specs=[
            pl.BlockSpec((b, d), lambda l, j: (0, 0)),
            pl.BlockSpec((1, d, d), lambda l, j: (l, 0, j)),
            pl.BlockSpec((1, 1, d), lambda l, j: (l, 0, j)),
        ],
        out_specs=pl.BlockSpec((1, b, d), lambda l, j: (l, 0, j)),
        out_shape=jax.ShapeDtypeStruct((depth, b, nd), F32),
        name="ada_mod",
    )(c, w_ada, b_ada.reshape(depth, 1, nd))


def _front_kernel(x_ref, mod_ref, nrm_ref, w_in_ref, wpool_ref, pscale_ref, sgn_ref, wcat_ref,
                  sbias_ref, mnp_ref, mng_ref, q_ref, k_ref, v_ref, pg_ref, e_ref, *, tm, sbw, pw):
    s = pl.program_id(1)
    sub = FRONT_SUB
    c0 = 3 * sbw
    heads_per_group = LANES // HEAD_DIM
    gdim = pw // len(POOL_WINDOWS)
    hd = pw // SG_HEADS
    lane_g = lax.broadcasted_iota(jnp.int32, (sub, LANES), 1)
    lane_e = lax.broadcasted_iota(jnp.int32, (sub + POOL_HALO, pw), 1)
    lane = lax.broadcasted_iota(jnp.int32, (sub, pw), 1)
    row = lax.broadcasted_iota(jnp.int32, (sub, pw), 0)
    lane_c = lax.broadcasted_iota(jnp.int32, (CHUNK, pw), 1)
    wlen = jnp.where(lane < gdim, POOL_WINDOWS[0],
                     jnp.where(lane < 2 * gdim, POOL_WINDOWS[1],
                               jnp.where(lane < 3 * gdim, POOL_WINDOWS[2], POOL_WINDOWS[3])))

    @pl.when(s == 0)
    def _():
        e_ref[0:POOL_HALO, :] = jnp.zeros((POOL_HALO, pw), F32)

    @pl.when(s > 0)
    def _():
        e_ref[0:POOL_HALO, :] = e_ref[tm:tm + POOL_HALO, :]

    def normed(r):
        x = x_ref[0, r * sub:(r + 1) * sub, :]
        h = _rms(x, nrm_ref[...]) * (1.0 + mod_ref[0, 1:2, :]) + mod_ref[0, 0:1, :]
        return h.astype(BF16)

    def project(r, hb):
        rows = slice(r * sub, (r + 1) * sub)
        mixer_in = jnp.dot(hb, w_in_ref[:, c0:], preferred_element_type=F32)
        for t, ref in enumerate((q_ref, k_ref, v_ref)):
            y = jnp.dot(hb, w_in_ref[:, t * sbw:(t + 1) * sbw], preferred_element_type=F32)
            for j in range(sbw // LANES):
                grp = y[:, j * LANES:(j + 1) * LANES]
                for hh in range(heads_per_group):
                    own = (lane_g >= hh * HEAD_DIM) & (lane_g < (hh + 1) * HEAD_DIM)
                    fill = jnp.where(lane_g == _spare_lane(hh), 1.0, 0.0) if t == 0 else 0.0
                    ref[0, heads_per_group * j + hh, rows, :] = jnp.where(own, grp, fill).astype(BF16)
        return mixer_in

    def pool(r, p):
        e_ref[POOL_HALO + r * sub:POOL_HALO + (r + 1) * sub, :] = p
        e = e_ref[r * sub:(r + 1) * sub + POOL_HALO, :]
        s2 = e + pltpu.roll(e, 1, 0)
        s4 = s2 + pltpu.roll(s2, 2, 0)
        s8 = s4 + pltpu.roll(s4, 4, 0)
        s16 = s8 + pltpu.roll(s8, 8, 0)
        wsum = jnp.where(lane_e < gdim, s2,
                         jnp.where(lane_e < 2 * gdim, s4, jnp.where(lane_e < 3 * gdim, s8, s16)))
        wsum = wsum[POOL_HALO:, :]
        count = jnp.minimum(s * tm + r * sub + row + 1, wlen).astype(F32)
        pooled = wsum / count - p
        mixed = jnp.dot(pooled.astype(BF16), wpool_ref[...], preferred_element_type=F32) * pscale_ref[...]
        pg_ref[0, r * sub:(r + 1) * sub, 0:pw] = _rms(mixed, mnp_ref[...]).astype(BF16)

    def gate(r, u, vg):
        ug = jax.nn.gelu(u)
        vge = jax.nn.gelu(vg)
        mu = jnp.mean(vge, axis=-1, keepdims=True)
        var = jnp.mean(jnp.square(vge - mu), axis=-1, keepdims=True)
        vn = ((vge - mu) * lax.rsqrt(var + EPS) * sgn_ref[...]).astype(BF16)
        zero = jnp.zeros((CHUNK, pw), BF16)
        outs = []
        for c in range(sub // CHUNK):
            vc = vn[c * CHUNK:(c + 1) * CHUNK, :]
            stacked = jnp.concatenate(
                [jnp.where((lane_c >= hh * hd) & (lane_c < (hh + 1) * hd), vc, zero) for hh in range(SG_HEADS)],
                axis=0)
            outs.append(jnp.dot(wcat_ref[...], stacked, preferred_element_type=F32) + sbias_ref[...])
        g_out = ug * jnp.concatenate(outs, axis=0)
        pg_ref[0, r * sub:(r + 1) * sub, pw:2 * pw] = _rms(g_out, mng_ref[...]).astype(BF16)

    n_sub = tm // sub
    hbs = [normed(r) for r in range(n_sub)]
    for r in range(n_sub):
        mixer_in = project(r, hbs[r])
        pool(r, mixer_in[:, :pw])
        gate(r, mixer_in[:, pw:2 * pw], mixer_in[:, 2 * pw:])


def _front(x, mod, nrm, w_in_b, wpool_bd, pscale, sgn, wcat, sbias, mnp, mng, *, layer, tm, sbw, pw):
    b, s, d = x.shape
    nh = sbw // HEAD_DIM
    in_w = w_in_b.shape[2]
    const = lambda *shape: pl.BlockSpec(shape, lambda bi, si: (0,) * len(shape))
    stacked = lambda *shape: pl.BlockSpec((None,) + shape, lambda bi, si: (layer,) + (0,) * len(shape))
    qkv_spec = pl.BlockSpec((1, nh, tm, LANES), lambda bi, si: (bi, 0, si, 0))
    qkv_shape = jax.ShapeDtypeStruct((b, nh, s, LANES), BF16)
    return pl.pallas_call(
        functools.partial(_front_kernel, tm=tm, sbw=sbw, pw=pw),
        grid=(b, s // tm),
        in_specs=[
            pl.BlockSpec((1, tm, d), lambda bi, si: (bi, si, 0)),
            pl.BlockSpec((1, N_MOD, d), lambda bi, si: (bi, 0, 0)),
            const(1, d), stacked(d, in_w), const(pw, pw), const(1, pw), const(1, pw),
            const(CHUNK, SG_HEADS * CHUNK), const(CHUNK, pw), const(1, pw), const(1, pw),
        ],
        out_specs=[qkv_spec, qkv_spec, qkv_spec,
                   pl.BlockSpec((1, tm, 2 * pw), lambda bi, si: (bi, si, 0))],
        out_shape=[qkv_shape, qkv_shape, qkv_shape, jax.ShapeDtypeStruct((b, s, 2 * pw), BF16)],
        scratch_shapes=[pltpu.VMEM((tm + POOL_HALO, pw), F32)],
        compiler_params=pltpu.CompilerParams(
            dimension_semantics=("arbitrary", "arbitrary"), vmem_limit_bytes=VMEM_LIMIT),
        name="front",
    )(x, mod, nrm, w_in_b, wpool_bd, pscale, sgn, wcat, sbias, mnp, mng)


TRI_WIDTH = 256


def _neg_abs(z):
    sign = jnp.uint32(0x80000000)
    return lax.bitcast_convert_type(lax.bitcast_convert_type(z, jnp.uint32) | sign, F32)


def _mixer_ffn_kernel(q_ref, kp_ref, kc_ref, vp_ref, vc_ref, pg_ref, x_ref, moda_ref, modf_ref, lc_ref, tri_ref,
                      mna_ref, wout_ref, nrm_ref, wg_ref, wu_ref, wd_ref, fn_ref, o_ref, x1_scr, res_scr,
                      *, tq, nh, n_seq, n_tiles, final):
    j = pl.program_id(0)
    i = jnp.minimum(j, n_tiles - 1) % n_seq
    nqb = tq // Q_BLOCK
    edge = KV_SPAN - Q_BLOCK
    group = LANES // HEAD_DIM
    assert KV_SPAN - 2 * Q_BLOCK + Q_BLOCK == 2 * TRI_WIDTH

    @pl.when(j == 0)
    def _():
        x1_scr[...] = jnp.zeros(x1_scr.shape, F32)

    res_scr[...] = x1_scr[...]

    sub = FFN_SUB
    subs = [slice(r * sub, (r + 1) * sub) for r in range(tq // sub)]
    dff = wg_ref.shape[1]
    d_model = wd_ref.shape[1]
    n_ct = dff // FFN_COLS
    hbs, acts = {}, {}

    def ffn_norm(r):
        h = _rms(res_scr[subs[r], :], nrm_ref[...]) * (1.0 + modf_ref[0, 4:5, :]) + modf_ref[0, 3:4, :]
        hbs[r] = h.astype(BF16)

    def ffn_gate_up(r, c):
        cols = slice(c * FFN_COLS, (c + 1) * FFN_COLS)
        gate = jnp.dot(hbs[r], wg_ref[:, cols], preferred_element_type=F32)
        up = jnp.dot(hbs[r], wu_ref[:, cols], preferred_element_type=F32)
        acts[r, c] = (gate / (1.0 + jnp.exp(-gate)) * up).astype(BF16)

    def ffn_down(r, c):
        cols = slice(c * FFN_COLS, (c + 1) * FFN_COLS)
        act = jnp.concatenate([acts[r, k] for k in range(n_ct)], axis=1)
        y = jnp.dot(act, wd_ref[:, cols], preferred_element_type=F32)
        out = res_scr[subs[r], cols] + modf_ref[0, 5:6, cols] * y
        o_ref[0, subs[r], cols] = out

    def ffn_finish(r):
        if final:
            o_ref[0, subs[r], :] = _rms(o_ref[0, subs[r], :], fn_ref[...])

    pieces = []
    for r in range(len(subs)):
        pieces += [functools.partial(ffn_gate_up, r, c) for c in range(n_ct)]
        pieces += [functools.partial(ffn_down, r, c) for c in range(d_model // FFN_COLS)]
        pieces.append(functools.partial(ffn_finish, r))

    kb_prev = jnp.where(i == 0, NEG_BIG, 0.0).astype(BF16)
    prev_bias = [kb_prev * lc_ref[hh:hh + 1, :] for hh in range(group)]

    def window(prev_ref, cur_ref, h, off):
        return prev_ref[0, h, off:, :], cur_ref[0, h, 0:off + Q_BLOCK, :]

    newest = (lax.broadcasted_iota(jnp.int32, (Q_BLOCK, Q_BLOCK), 1)
              < lax.broadcasted_iota(jnp.int32, (Q_BLOCK, Q_BLOCK), 0))
    half = TRI_WIDTH - Q_BLOCK

    def scores(item):
        qb, h = divmod(item, nh)
        off = qb * Q_BLOCK
        k_prev, k_cur = window(kp_ref, kc_ref, h, off)
        keys = jnp.concatenate([k_prev + prev_bias[h % group], k_cur], axis=0)
        z = lax.dot_general(q_ref[0, h, off:off + Q_BLOCK, :], keys, (((1,), (1,)), ((), ())),
                            preferred_element_type=F32)
        return jnp.concatenate([jnp.where(newest, z[:, edge:], z[:, :Q_BLOCK]), z[:, Q_BLOCK:edge]], axis=1)

    def softplus(z):
        n = jnp.maximum(z, 0.0) + jnp.log(1.0 + jnp.exp2(_neg_abs(z))) * LOG2E
        nb = n.astype(BF16)
        n_new = jnp.where(newest, n[:, :Q_BLOCK], 0.0)
        nb_new = n_new.astype(BF16)
        nb_old = nb[:, :Q_BLOCK] - nb_new
        tot_new = jnp.sum(n_new, axis=-1, keepdims=True)
        tot_mid = jnp.sum(n[:, Q_BLOCK + half:], axis=-1, keepdims=True)
        chunks = (jnp.concatenate([nb_old, nb[:, Q_BLOCK:Q_BLOCK + half]], axis=1), nb[:, Q_BLOCK + half:], nb_new)
        return z - n, chunks, (tot_mid + tot_new, tot_new)

    def log_weight(state):
        dmn, (c_first, c_mid, c_new), (tot_first, tot_mid) = state
        l_first = jnp.dot(c_first, tri_ref[...], preferred_element_type=F32) + tot_first
        l_mid = jnp.dot(c_mid, tri_ref[...], preferred_element_type=F32) + tot_mid
        l_new = jnp.dot(c_new, tri_ref[0:Q_BLOCK, 0:Q_BLOCK], preferred_element_type=F32)
        later = jnp.concatenate([jnp.where(newest, l_new, l_first[:, :Q_BLOCK]), l_first[:, Q_BLOCK:], l_mid],
                                axis=1)
        return dmn - later

    def weights(arg):
        w = jnp.exp2(arg)
        wb = w.astype(BF16)
        w_new = jnp.where(newest, w[:, :Q_BLOCK], 0.0).astype(BF16)
        return jnp.concatenate([wb[:, :Q_BLOCK] - w_new, wb[:, Q_BLOCK:], w_new], axis=1)

    def weighted_values(item, w):
        qb, h = divmod(item, nh)
        values = jnp.concatenate(window(vp_ref, vc_ref, h, qb * Q_BLOCK), axis=0)
        return jnp.dot(w, values, preferred_element_type=F32)

    def project(qb, pvs):
        a = jnp.concatenate([sum(pvs[g * group + 1:(g + 1) * group], pvs[g * group])
                             for g in range(nh // group)], axis=1)
        rows = slice(qb * Q_BLOCK, (qb + 1) * Q_BLOCK)
        an = _rms(a, mna_ref[...]).astype(BF16)
        merged = jnp.concatenate([an, pg_ref[0, rows, :]], axis=1)
        y = jnp.dot(merged, wout_ref[...], preferred_element_type=F32)
        x1_scr[rows, :] = x_ref[0, rows, :] + moda_ref[0, 2:3, :] * y

    n_items = nqb * nh
    per_step = -(-len(pieces) // n_items)
    zs, states, ws, pvs = {}, {}, {}, {}
    for step in range(-2, n_items + 2):
        if step == 0:
            for r in range(len(subs)):
                ffn_norm(r)
        for _ in range(per_step):
            if pieces and step >= 0:
                pieces.pop(0)()
        if 0 <= step + 2 < n_items:
            zs[step + 2] = scores(step + 2)
        if 0 <= step < n_items:
            states[step] = softplus(zs.pop(step))
        if 0 <= step - 1 < n_items:
            ws[step - 1] = weights(log_weight(states.pop(step - 1)))
        if 0 <= step - 2 < n_items:
            item = step - 2
            pvs[item] = weighted_values(item, ws.pop(item))
            if item % nh == nh - 1:
                qb = item // nh
                project(qb, [pvs.pop(qb * nh + h) for h in range(nh)])
    assert not pieces


def _mixer_ffn(q, k, v, pg, x, mod, lane_consts, tri, mna, wout_b, nrm, wgu_b, wd_b, fn, *, layer, tq, final):
    b, s, d = x.shape
    nh = q.shape[1]
    mixw = wout_b.shape[1]
    dff = wd_b.shape[1]
    n_seq = s // tq
    n_tiles = b * n_seq
    once = lambda *shape: pl.BlockSpec(shape, lambda j: (0,) * len(shape), pipeline_mode=pl.Buffered(1))
    stacked = lambda *shape, col=0: pl.BlockSpec((None,) + shape, lambda j: (layer, 0, col),
                                                 pipeline_mode=pl.Buffered(1))
    mixer_tile = lambda j: jnp.minimum(j, n_tiles - 1)
    ffn_tile = lambda j: jnp.maximum(j - 1, 0)
    cur = pl.BlockSpec((1, nh, tq, LANES), lambda j: (mixer_tile(j) // n_seq, 0, mixer_tile(j) % n_seq, 0))
    prev = pl.BlockSpec((1, nh, tq, LANES),
                        lambda j: (mixer_tile(j) // n_seq, 0, jnp.maximum(mixer_tile(j) % n_seq - 1, 0), 0))
    rows_a = lambda width: pl.BlockSpec((1, tq, width),
                                        lambda j: (mixer_tile(j) // n_seq, mixer_tile(j) % n_seq, 0))
    return pl.pallas_call(
        functools.partial(_mixer_ffn_kernel, tq=tq, nh=nh, n_seq=n_seq, n_tiles=n_tiles, final=final),
        grid=(n_tiles + 1,),
        in_specs=[
            cur, prev, cur, prev, cur, rows_a(pg.shape[2]), rows_a(d),
            pl.BlockSpec((1, N_MOD, d), lambda j: (mixer_tile(j) // n_seq, 0, 0)),
            pl.BlockSpec((1, N_MOD, d), lambda j: (ffn_tile(j) // n_seq, 0, 0)),
            once(*lane_consts.shape), once(*tri.shape), once(1, nh * HEAD_DIM), stacked(mixw, d),
            once(1, d), stacked(d, dff, col=0), stacked(d, dff, col=1), stacked(dff, d), once(1, d),
        ],
        out_specs=pl.BlockSpec((1, tq, d), lambda j: (ffn_tile(j) // n_seq, ffn_tile(j) % n_seq, 0)),
        out_shape=jax.ShapeDtypeStruct((b, s, d), F32),
        scratch_shapes=[pltpu.VMEM((tq, d), F32), pltpu.VMEM((tq, d), F32)],
        compiler_params=pltpu.CompilerParams(dimension_semantics=("arbitrary",), vmem_limit_bytes=VMEM_LIMIT),
        name="mixer_ffn",
    )(q, k, k, v, v, pg, x, mod, mod, lane_consts, tri, mna, wout_b, nrm, wgu_b, wgu_b, wd_b, fn)


def _lane_constants():
    rows = np.zeros((16, LANES), np.float32)
    for hh in range(LANES // HEAD_DIM):
        rows[hh, _spare_lane(hh)] = 1.0
    return rows


def _suffix_sum_matrix(n):
    j = np.arange(n)
    return (j[:, None] > j[None, :]).astype(np.float32)


def kernel(x, c, w_ada, b_ada, norm_mix_in, w_in, w_pool, pool_scale, sg_norm, w_s, b_s, mix_norm, w_out,
           norm_ffn_in, w_gate_up, w_down, final_norm):
    b, s, d = x.shape
    depth = w_ada.shape[0]
    pw = pool_scale.shape[1]
    sbw = (w_in.shape[2] - 3 * pw) // 3
    tm = min(ROW_TILE, s)
    tf = min(FRONT_TILE, s)
    assert s % tm == 0 and tm % Q_BLOCK == 0 and tm >= SB_WINDOW and sbw % LANES == 0
    assert s % tf == 0 and tf % FRONT_SUB == 0 and FRONT_SUB % CHUNK == 0
    assert tm % FFN_SUB == 0

    mod_all = _ada(c, w_ada, b_ada).reshape(depth, b, N_MOD, d)

    q_scale = jnp.concatenate([jnp.full((sbw,), LOG2E * HEAD_DIM ** -0.5, F32),
                               jnp.ones((w_in.shape[2] - sbw,), F32)])
    w_in_b = (w_in * q_scale).astype(BF16)
    groups = w_pool.shape[1]
    gdim = w_pool.shape[2]
    eye = jnp.eye(groups, dtype=F32)
    wpool_bd = (w_pool[:, :, :, None, :] * eye[None, :, None, :, None]).reshape(
        depth, groups * gdim, groups * gdim).astype(BF16)
    causal = jnp.tril(jnp.ones((CHUNK, CHUNK), dtype=bool))
    ws_c = jnp.where(causal[None, None], w_s, 0.0)
    wcat = ws_c.transpose(0, 2, 1, 3).reshape(depth, CHUNK, SG_HEADS * CHUNK).astype(BF16)
    sbias = jnp.repeat(b_s.transpose(0, 2, 1), pw // SG_HEADS, axis=2)
    w_out_b = w_out.astype(BF16)
    wgu_b = w_gate_up.astype(BF16)
    wd_b = w_down.astype(BF16)
    lane_consts = jnp.asarray(_lane_constants(), dtype=BF16)
    tri = jnp.asarray(_suffix_sum_matrix(TRI_WIDTH), dtype=BF16)
    row = lambda v: v.reshape(1, -1)

    for l in range(depth):
        mod = mod_all[l]
        q, k, v, pg = _front(x, mod, row(norm_mix_in[l]), w_in_b, wpool_bd[l], row(pool_scale[l]),
                             row(sg_norm[l]), wcat[l], sbias[l], row(mix_norm[l, sbw:sbw + pw]),
                             row(mix_norm[l, sbw + pw:]), layer=l, tm=tf, sbw=sbw, pw=pw)
        x = _mixer_ffn(q, k, v, pg, x, mod, lane_consts, tri, row(mix_norm[l, :sbw]), w_out_b,
                       row(norm_ffn_in[l]), wgu_b, wd_b, row(final_norm),
                       layer=l, tq=tm, final=(l == depth - 1))
    return x
```

```python
import functools

import numpy as np
import jax
import jax.numpy as jnp
from jax import lax
from jax.experimental import pallas as pl
from jax.experimental.pallas import tpu as pltpu

F32 = jnp.float32
BF16 = jnp.bfloat16

HEAD_DIM = 64
Q_BLOCK = 128
SB_WINDOW = 512
KV_SPAN = SB_WINDOW + Q_BLOCK
POOL_WINDOWS = (2, 4, 8, 16)
POOL_HALO = 16
CHUNK = 128
SG_HEADS = 4
N_MOD = 6
EPS = 1e-6
LANES = 128
NEG_BIG = -1e30
LOG2E = 1.4426950408889634

ROW_TILE = 512
FRONT_TILE = 1024
FRONT_SUB = 256
FFN_SUB = 256
FFN_COLS = 256
VMEM_LIMIT = 56 * 1024 * 1024


def _spare_lane(head_in_group):
    return HEAD_DIM if head_in_group == 0 else 0


def _rms(x, gain):
    return x * lax.rsqrt(jnp.mean(x * x, axis=-1, keepdims=True) + EPS) * gain


def _ada_kernel(c_ref, w_ref, b_ref, o_ref):
    c = c_ref[...]
    c_act = c / (1.0 + jnp.exp(-c))
    o_ref[0] = jnp.dot(c_act, w_ref[0], preferred_element_type=F32,
                       precision=lax.Precision.HIGHEST) + b_ref[0]


def _ada(c, w_ada, b_ada):
    depth, d, nd = w_ada.shape
    b = c.shape[0]
    return pl.pallas_call(
        _ada_kernel,
        grid=(depth, nd // d),
        in_specs=[
            pl.BlockSpec((b, d), lambda l, j: (0, 0)),
            pl.BlockSpec((1, d, d), lambda l, j: (l, 0, j)),
            pl.BlockSpec((1, 1, d), lambda l, j: (l, 0, j)),
        ],
        out_specs=pl.BlockSpec((1, b, d), lambda l, j: (l, 0, j)),
        out_shape=jax.ShapeDtypeStruct((depth, b, nd), F32),
        name="ada_mod",
    )(c, w_ada, b_ada.reshape(depth, 1, nd))


def _front_kernel(x_ref, mod_ref, nrm_ref, w_in_ref, wpool_ref, pscale_ref, sgn_ref, wcat_ref,
                  sbias_ref, mnp_ref, mng_ref, q_ref, k_ref, v_ref, pg_ref, e_ref, *, tm, sbw, pw):
    s = pl.program_id(1)
    sub = FRONT_SUB
    c0 = 3 * sbw
    heads_per_group = LANES // HEAD_DIM
    gdim = pw // len(POOL_WINDOWS)
    hd = pw // SG_HEADS
    lane_g = lax.broadcasted_iota(jnp.int32, (sub, LANES), 1)
    lane_e = lax.broadcasted_iota(jnp.int32, (sub + POOL_HALO, pw), 1)
    lane = lax.broadcasted_iota(jnp.int32, (sub, pw), 1)
    row = lax.broadcasted_iota(jnp.int32, (sub, pw), 0)
    lane_c = lax.broadcasted_iota(jnp.int32, (CHUNK, pw), 1)
    wlen = jnp.where(lane < gdim, POOL_WINDOWS[0],
                     jnp.where(lane < 2 * gdim, POOL_WINDOWS[1],
                               jnp.where(lane < 3 * gdim, POOL_WINDOWS[2], POOL_WINDOWS[3])))

    @pl.when(s == 0)
    def _():
        e_ref[0:POOL_HALO, :] = jnp.zeros((POOL_HALO, pw), F32)

    @pl.when(s > 0)
    def _():
        e_ref[0:POOL_HALO, :] = e_ref[tm:tm + POOL_HALO, :]

    def normed(r):
        x = x_ref[0, r * sub:(r + 1) * sub, :]
        h = _rms(x, nrm_ref[...]) * (1.0 + mod_ref[0, 1:2, :]) + mod_ref[0, 0:1, :]
        return h.astype(BF16)

    def project(r, hb):
        rows = slice(r * sub, (r + 1) * sub)
        mixer_in = jnp.dot(hb, w_in_ref[:, c0:], preferred_element_type=F32)
        for t, ref in enumerate((q_ref, k_ref, v_ref)):
            y = jnp.dot(hb, w_in_ref[:, t * sbw:(t + 1) * sbw], preferred_element_type=F32)
            for j in range(sbw // LANES):
                grp = y[:, j * LANES:(j + 1) * LANES]
                for hh in range(heads_per_group):
                    own = (lane_g >= hh * HEAD_DIM) & (lane_g < (hh + 1) * HEAD_DIM)
                    fill = jnp.where(lane_g == _spare_lane(hh), 1.0, 0.0) if t == 0 else 0.0
                    ref[0, heads_per_group * j + hh, rows, :] = jnp.where(own, grp, fill).astype(BF16)
        return mixer_in

    def pool(r, p):
        e_ref[POOL_HALO + r * sub:POOL_HALO + (r + 1) * sub, :] = p
        e = e_ref[r * sub:(r + 1) * sub + POOL_HALO, :]
        s2 = e + pltpu.roll(e, 1, 0)
        s4 = s2 + pltpu.roll(s2, 2, 0)
        s8 = s4 + pltpu.roll(s4, 4, 0)
        s16 = s8 + pltpu.roll(s8, 8, 0)
        wsum = jnp.where(lane_e < gdim, s2,
                         jnp.where(lane_e < 2 * gdim, s4, jnp.where(lane_e < 3 * gdim, s8, s16)))
        wsum = wsum[POOL_HALO:, :]
        count = jnp.minimum(s * tm + r * sub + row + 1, wlen).astype(F32)
        pooled = wsum / count - p
        mixed = jnp.dot(pooled.astype(BF16), wpool_ref[...], preferred_element_type=F32) * pscale_ref[...]
        pg_ref[0, r * sub:(r + 1) * sub, 0:pw] = _rms(mixed, mnp_ref[...]).astype(BF16)

    def gate(r, u, vg):
        ug = jax.nn.gelu(u)
        vge = jax.nn.gelu(vg)
        mu = jnp.mean(vge, axis=-1, keepdims=True)
        var = jnp.mean(jnp.square(vge - mu), axis=-1, keepdims=True)
        vn = ((vge - mu) * lax.rsqrt(var + EPS) * sgn_ref[...]).astype(BF16)
        zero = jnp.zeros((CHUNK, pw), BF16)
        outs = []
        for c in range(sub // CHUNK):
            vc = vn[c * CHUNK:(c + 1) * CHUNK, :]
            stacked = jnp.concatenate(
                [jnp.where((lane_c >= hh * hd) & (lane_c < (hh + 1) * hd), vc, zero) for hh in range(SG_HEADS)],
                axis=0)
            outs.append(jnp.dot(wcat_ref[...], stacked, preferred_element_type=F32) + sbias_ref[...])
        g_out = ug * jnp.concatenate(outs, axis=0)
        pg_ref[0, r * sub:(r + 1) * sub, pw:2 * pw] = _rms(g_out, mng_ref[...]).astype(BF16)

    n_sub = tm // sub
    hbs = [normed(r) for r in range(n_sub)]
    for r in range(n_sub):
        mixer_in = project(r, hbs[r])
        pool(r, mixer_in[:, :pw])
        gate(r, mixer_in[:, pw:2 * pw], mixer_in[:, 2 * pw:])


def _front(x, mod, nrm, w_in_b, wpool_bd, pscale, sgn, wcat, sbias, mnp, mng, *, layer, tm, sbw, pw):
    b, s, d = x.shape
    nh = sbw // HEAD_DIM
    in_w = w_in_b.shape[2]
    const = lambda *shape: pl.BlockSpec(shape, lambda bi, si: (0,) * len(shape))
    stacked = lambda *shape: pl.BlockSpec((None,) + shape, lambda bi, si: (layer,) + (0,) * len(shape))
    qkv_spec = pl.BlockSpec((1, nh, tm, LANES), lambda bi, si: (bi, 0, si, 0))
    qkv_shape = jax.ShapeDtypeStruct((b, nh, s, LANES), BF16)
    return pl.pallas_call(
        functools.partial(_front_kernel, tm=tm, sbw=sbw, pw=pw),
        grid=(b, s // tm),
        in_specs=[
            pl.BlockSpec((1, tm, d), lambda bi, si: (bi, si, 0)),
            pl.BlockSpec((1, N_MOD, d), lambda bi, si: (bi, 0, 0)),
            const(1, d), stacked(d, in_w), const(pw, pw), const(1, pw), const(1, pw),
            const(CHUNK, SG_HEADS * CHUNK), const(CHUNK, pw), const(1, pw), const(1, pw),
        ],
        out_specs=[qkv_spec, qkv_spec, qkv_spec,
                   pl.BlockSpec((1, tm, 2 * pw), lambda bi, si: (bi, si, 0))],
        out_shape=[qkv_shape, qkv_shape, qkv_shape, jax.ShapeDtypeStruct((b, s, 2 * pw), BF16)],
        scratch_shapes=[pltpu.VMEM((tm + POOL_HALO, pw), F32)],
        compiler_params=pltpu.CompilerParams(
            dimension_semantics=("arbitrary", "arbitrary"), vmem_limit_bytes=VMEM_LIMIT),
        name="front",
    )(x, mod, nrm, w_in_b, wpool_bd, pscale, sgn, wcat, sbias, mnp, mng)


TRI_WIDTH = 256


def _neg_abs(z):
    sign = jnp.uint32(0x80000000)
    return lax.bitcast_convert_type(lax.bitcast_convert_type(z, jnp.uint32) | sign, F32)


def _mixer_ffn_kernel(q_ref, kp_ref, kc_ref, vp_ref, vc_ref, pg_ref, x_ref, moda_ref, modf_ref, lc_ref, tri_ref,
                      mna_ref, wout_ref, nrm_ref, wg_ref, wu_ref, wd_ref, fn_ref, o_ref, x1_scr, res_scr,
                      *, tq, nh, n_seq, n_tiles, final):
    j = pl.program_id(0)
    i = jnp.minimum(j, n_tiles - 1) % n_seq
    nqb = tq // Q_BLOCK
    edge = KV_SPAN - Q_BLOCK
    group = LANES // HEAD_DIM
    assert KV_SPAN - 2 * Q_BLOCK + Q_BLOCK == 2 * TRI_WIDTH

    @pl.when(j == 0)
    def _():
        x1_scr[...] = jnp.zeros(x1_scr.shape, F32)

    res_scr[...] = x1_scr[...]

    sub = FFN_SUB
    subs = [slice(r * sub, (r + 1) * sub) for r in range(tq // sub)]
    dff = wg_ref.shape[1]
    d_model = wd_ref.shape[1]
    n_ct = dff // FFN_COLS
    hbs, acts = {}, {}

    def ffn_norm(r):
        h = _rms(res_scr[subs[r], :], nrm_ref[...]) * (1.0 + modf_ref[0, 4:5, :]) + modf_ref[0, 3:4, :]
        hbs[r] = h.astype(BF16)

    def ffn_gate_up(r, c):
        cols = slice(c * FFN_COLS, (c + 1) * FFN_COLS)
        gate = jnp.dot(hbs[r], wg_ref[:, cols], preferred_element_type=F32)
        up = jnp.dot(hbs[r], wu_ref[:, cols], preferred_element_type=F32)
        acts[r, c] = (gate / (1.0 + jnp.exp(-gate)) * up).astype(BF16)

    def ffn_down(r, c):
        cols = slice(c * FFN_COLS, (c + 1) * FFN_COLS)
        act = jnp.concatenate([acts[r, k] for k in range(n_ct)], axis=1)
        y = jnp.dot(act, wd_ref[:, cols], preferred_element_type=F32)
        out = res_scr[subs[r], cols] + modf_ref[0, 5:6, cols] * y
        o_ref[0, subs[r], cols] = out

    def ffn_finish(r):
        if final:
            o_ref[0, subs[r], :] = _rms(o_ref[0, subs[r], :], fn_ref[...])

    pieces = []
    for r in range(len(subs)):
        pieces += [functools.partial(ffn_gate_up, r, c) for c in range(n_ct)]
        pieces += [functools.partial(ffn_down, r, c) for c in range(d_model // FFN_COLS)]
        pieces.append(functools.partial(ffn_finish, r))

    kb_prev = jnp.where(i == 0, NEG_BIG, 0.0).astype(BF16)
    prev_bias = [kb_prev * lc_ref[hh:hh + 1, :] for hh in range(group)]

    def window(prev_ref, cur_ref, h, off):
        return prev_ref[0, h, off:, :], cur_ref[0, h, 0:off + Q_BLOCK, :]

    newest = (lax.broadcasted_iota(jnp.int32, (Q_BLOCK, Q_BLOCK), 1)
              < lax.broadcasted_iota(jnp.int32, (Q_BLOCK, Q_BLOCK), 0))
    half = TRI_WIDTH - Q_BLOCK

    def scores(item):
        qb, h = divmod(item, nh)
        off = qb * Q_BLOCK
        k_prev, k_cur = window(kp_ref, kc_ref, h, off)
        keys = jnp.concatenate([k_prev + prev_bias[h % group], k_cur], axis=0)
        z = lax.dot_general(q_ref[0, h, off:off + Q_BLOCK, :], keys, (((1,), (1,)), ((), ())),
                            preferred_element_type=F32)
        return jnp.concatenate([jnp.where(newest, z[:, edge:], z[:, :Q_BLOCK]), z[:, Q_BLOCK:edge]], axis=1)

    def softplus(z):
        n = jnp.maximum(z, 0.0) + jnp.log(1.0 + jnp.exp2(_neg_abs(z))) * LOG2E
        nb = n.astype(BF16)
        n_new = jnp.where(newest, n[:, :Q_BLOCK], 0.0)
        nb_new = n_new.astype(BF16)
        nb_old = nb[:, :Q_BLOCK] - nb_new
        tot_new = jnp.sum(n_new, axis=-1, keepdims=True)
        tot_mid = jnp.sum(n[:, Q_BLOCK + half:], axis=-1, keepdims=True)
        chunks = (jnp.concatenate([nb_old, nb[:, Q_BLOCK:Q_BLOCK + half]], axis=1), nb[:, Q_BLOCK + half:], nb_new)
        return z - n, chunks, (tot_mid + tot_new, tot_new)

    def log_weight(state):
        dmn, (c_first, c_mid, c_new), (tot_first, tot_mid) = state
        l_first = jnp.dot(c_first, tri_ref[...], preferred_element_type=F32) + tot_first
        l_mid = jnp.dot(c_mid, tri_ref[...], preferred_element_type=F32) + tot_mid
        l_new = jnp.dot(c_new, tri_ref[0:Q_BLOCK, 0:Q_BLOCK], preferred_element_type=F32)
        later = jnp.concatenate([jnp.where(newest, l_new, l_first[:, :Q_BLOCK]), l_first[:, Q_BLOCK:], l_mid],
                                axis=1)
        return dmn - later

    def weights(arg):
        w = jnp.exp2(arg)
        wb = w.astype(BF16)
        w_new = jnp.where(newest, w[:, :Q_BLOCK], 0.0).astype(BF16)
        return jnp.concatenate([wb[:, :Q_BLOCK] - w_new, wb[:, Q_BLOCK:], w_new], axis=1)

    def weighted_values(item, w):
        qb, h = divmod(item, nh)
        values = jnp.concatenate(window(vp_ref, vc_ref, h, qb * Q_BLOCK), axis=0)
        return jnp.dot(w, values, preferred_element_type=F32)

    def project(qb, pvs):
        a = jnp.concatenate([sum(pvs[g * group + 1:(g + 1) * group], pvs[g * group])
                             for g in range(nh // group)], axis=1)
        rows = slice(qb * Q_BLOCK, (qb + 1) * Q_BLOCK)
        an = _rms(a, mna_ref[...]).astype(BF16)
        merged = jnp.concatenate([an, pg_ref[0, rows, :]], axis=1)
        y = jnp.dot(merged, wout_ref[...], preferred_element_type=F32)
        x1_scr[rows, :] = x_ref[0, rows, :] + moda_ref[0, 2:3, :] * y

    n_items = nqb * nh
    per_step = -(-len(pieces) // n_items)
    zs, states, ws, pvs = {}, {}, {}, {}
    for step in range(-1, n_items + 2):
        if step == 0:
            for r in range(len(subs)):
                ffn_norm(r)
        for _ in range(per_step):
            if pieces and step >= 0:
                pieces.pop(0)()
        if 0 <= step + 1 < n_items:
            zs[step + 1] = scores(step + 1)
        if 0 <= step < n_items:
            states[step] = softplus(zs.pop(step))
        if 0 <= step - 1 < n_items:
            ws[step - 1] = weights(log_weight(states.pop(step - 1)))
        if 0 <= step - 2 < n_items:
            item = step - 2
            pvs[item] = weighted_values(item, ws.pop(item))
            if item % nh == nh - 1:
                qb = item // nh
                project(qb, [pvs.pop(qb * nh + h) for h in range(nh)])
    assert not pieces


def _mixer_ffn(q, k, v, pg, x, mod, lane_consts, tri, mna, wout_b, nrm, wgu_b, wd_b, fn, *, layer, tq, final):
    b, s, d = x.shape
    nh = q.shape[1]
    mixw = wout_b.shape[1]
    dff = wd_b.shape[1]
    n_seq = s // tq
    n_tiles = b * n_seq
    once = lambda *shape: pl.BlockSpec(shape, lambda j: (0,) * len(shape), pipeline_mode=pl.Buffered(1))
    stacked = lambda *shape, col=0: pl.BlockSpec((None,) + shape, lambda j: (layer, 0, col),
                                                 pipeline_mode=pl.Buffered(1))
    mixer_tile = lambda j: jnp.minimum(j, n_tiles - 1)
    ffn_tile = lambda j: jnp.maximum(j - 1, 0)
    cur = pl.BlockSpec((1, nh, tq, LANES), lambda j: (mixer_tile(j) // n_seq, 0, mixer_tile(j) % n_seq, 0))
    prev = pl.BlockSpec((1, nh, tq, LANES),
                        lambda j: (mixer_tile(j) // n_seq, 0, jnp.maximum(mixer_tile(j) % n_seq - 1, 0), 0))
    rows_a = lambda width: pl.BlockSpec((1, tq, width),
                                        lambda j: (mixer_tile(j) // n_seq, mixer_tile(j) % n_seq, 0))
    return pl.pallas_call(
        functools.partial(_mixer_ffn_kernel, tq=tq, nh=nh, n_seq=n_seq, n_tiles=n_tiles, final=final),
        grid=(n_tiles + 1,),
        in_specs=[
            cur, prev, cur, prev, cur, rows_a(pg.shape[2]), rows_a(d),
            pl.BlockSpec((1, N_MOD, d), lambda j: (mixer_tile(j) // n_seq, 0, 0)),
            pl.BlockSpec((1, N_MOD, d), lambda j: (ffn_tile(j) // n_seq, 0, 0)),
            once(*lane_consts.shape), once(*tri.shape), once(1, nh * HEAD_DIM), stacked(mixw, d),
            once(1, d), stacked(d, dff, col=0), stacked(d, dff, col=1), stacked(dff, d), once(1, d),
        ],
        out_specs=pl.BlockSpec((1, tq, d), lambda j: (ffn_tile(j) // n_seq, ffn_tile(j) % n_seq, 0)),
        out_shape=jax.ShapeDtypeStruct((b, s, d), F32),
        scratch_shapes=[pltpu.VMEM((tq, d), F32), pltpu.VMEM((tq, d), F32)],
        compiler_params=pltpu.CompilerParams(dimension_semantics=("arbitrary",), vmem_limit_bytes=VMEM_LIMIT),
        name="mixer_ffn",
    )(q, k, k, v, v, pg, x, mod, mod, lane_consts, tri, mna, wout_b, nrm, wgu_b, wgu_b, wd_b, fn)


def _lane_constants():
    rows = np.zeros((16, LANES), np.float32)
    for hh in range(LANES // HEAD_DIM):
        rows[hh, _spare_lane(hh)] = 1.0
    return rows


def _suffix_sum_matrix(n):
    j = np.arange(n)
    return (j[:, None] > j[None, :]).astype(np.float32)


def kernel(x, c, w_ada, b_ada, norm_mix_in, w_in, w_pool, pool_scale, sg_norm, w_s, b_s, mix_norm, w_out,
           norm_ffn_in, w_gate_up, w_down, final_norm):
    b, s, d = x.shape
    depth = w_ada.shape[0]
    pw = pool_scale.shape[1]
    sbw = (w_in.shape[2] - 3 * pw) // 3
    tm = min(ROW_TILE, s)
    tf = min(FRONT_TILE, s)
    assert s % tm == 0 and tm % Q_BLOCK == 0 and tm >= SB_WINDOW and sbw % LANES == 0
    assert s % tf == 0 and tf % FRONT_SUB == 0 and FRONT_SUB % CHUNK == 0
    assert tm % FFN_SUB == 0

    mod_all = _ada(c, w_ada, b_ada).reshape(depth, b, N_MOD, d)

    q_scale = jnp.concatenate([jnp.full((sbw,), LOG2E * HEAD_DIM ** -0.5, F32),
                               jnp.ones((w_in.shape[2] - sbw,), F32)])
    w_in_b = (w_in * q_scale).astype(BF16)
    groups = w_pool.shape[1]
    gdim = w_pool.shape[2]
    eye = jnp.eye(groups, dtype=F32)
    wpool_bd = (w_pool[:, :, :, None, :] * eye[None, :, None, :, None]).reshape(
        depth, groups * gdim, groups * gdim).astype(BF16)
    causal = jnp.tril(jnp.ones((CHUNK, CHUNK), dtype=bool))
    ws_c = jnp.where(causal[None, None], w_s, 0.0)
    wcat = ws_c.transpose(0, 2, 1, 3).reshape(depth, CHUNK, SG_HEADS * CHUNK).astype(BF16)
    sbias = jnp.repeat(b_s.transpose(0, 2, 1), pw // SG_HEADS, axis=2)
    w_out_b = w_out.astype(BF16)
    wgu_b = w_gate_up.astype(BF16)
    wd_b = w_down.astype(BF16)
    lane_consts = jnp.asarray(_lane_constants(), dtype=BF16)
    tri = jnp.asarray(_suffix_sum_matrix(TRI_WIDTH), dtype=BF16)
    row = lambda v: v.reshape(1, -1)

    for l in range(depth):
        mod = mod_all[l]
        q, k, v, pg = _front(x, mod, row(norm_mix_in[l]), w_in_b, wpool_bd[l], row(pool_scale[l]),
                             row(sg_norm[l]), wcat[l], sbias[l], row(mix_norm[l, sbw:sbw + pw]),
                             row(mix_norm[l, sbw + pw:]), layer=l, tm=tf, sbw=sbw, pw=pw)
        x = _mixer_ffn(q, k, v, pg, x, mod, lane_consts, tri, row(mix_norm[l, :sbw]), w_out_b,
                       row(norm_ffn_in[l]), wgu_b, wd_b, row(final_norm),
                       layer=l, tq=tm, final=(l == depth - 1))
    return x
```

```python
import functools

import numpy as np
import jax
import jax.numpy as jnp
from jax import lax
from jax.experimental import pallas as pl
from jax.experimental.pallas import tpu as pltpu

F32 = jnp.float32
BF16 = jnp.bfloat16

HEAD_DIM = 64
Q_BLOCK = 128
SB_WINDOW = 512
KV_SPAN = SB_WINDOW + Q_BLOCK
POOL_WINDOWS = (2, 4, 8, 16)
POOL_HALO = 16
CHUNK = 128
SG_HEADS = 4
N_MOD = 6
EPS = 1e-6
LANES = 128
NEG_BIG = -1e30
LOG2E = 1.4426950408889634

ROW_TILE = 512
FRONT_TILE = 1024
FRONT_SUB = 256
FFN_SUB = 256
FFN_COLS = 256
VMEM_LIMIT = 56 * 1024 * 1024


def _spare_lane(head_in_group):
    return HEAD_DIM if head_in_group == 0 else 0


def _rms(x, gain):
    return x * lax.rsqrt(jnp.mean(x * x, axis=-1, keepdims=True) + EPS) * gain


def _ada_kernel(c_ref, w_ref, b_ref, o_ref):
    c = c_ref[...]
    c_act = c / (1.0 + jnp.exp(-c))
    o_ref[0] = jnp.dot(c_act, w_ref[0], preferred_element_type=F32,
                       precision=lax.Precision.HIGHEST) + b_ref[0]


def _ada(c, w_ada, b_ada):
    depth, d, nd = w_ada.shape
    b = c.shape[0]
    return pl.pallas_call(
        _ada_kernel,
        grid=(depth, nd // d),
        in_specs=[
            pl.BlockSpec((b, d), lambda l, j: (0, 0)),
            pl.BlockSpec((1, d, d), lambda l, j: (l, 0, j)),
            pl.BlockSpec((1, 1, d), lambda l, j: (l, 0, j)),
        ],
        out_specs=pl.BlockSpec((1, b, d), lambda l, j: (l, 0, j)),
        out_shape=jax.ShapeDtypeStruct((depth, b, nd), F32),
        name="ada_mod",
    )(c, w_ada, b_ada.reshape(depth, 1, nd))


def _front_kernel(x_ref, mod_ref, nrm_ref, w_in_ref, wpool_ref, pscale_ref, sgn_ref, wcat_ref,
                  sbias_ref, mnp_ref, mng_ref, q_ref, k_ref, v_ref, pg_ref, e_ref, *, tm, sbw, pw):
    s = pl.program_id(1)
    sub = FRONT_SUB
    c0 = 3 * sbw
    heads_per_group = LANES // HEAD_DIM
    gdim = pw // len(POOL_WINDOWS)
    hd = pw // SG_HEADS
    lane_g = lax.broadcasted_iota(jnp.int32, (sub, LANES), 1)
    lane_e = lax.broadcasted_iota(jnp.int32, (sub + POOL_HALO, pw), 1)
    lane = lax.broadcasted_iota(jnp.int32, (sub, pw), 1)
    row = lax.broadcasted_iota(jnp.int32, (sub, pw), 0)
    lane_c = lax.broadcasted_iota(jnp.int32, (CHUNK, pw), 1)
    wlen = jnp.where(lane < gdim, POOL_WINDOWS[0],
                     jnp.where(lane < 2 * gdim, POOL_WINDOWS[1],
                               jnp.where(lane < 3 * gdim, POOL_WINDOWS[2], POOL_WINDOWS[3])))

    @pl.when(s == 0)
    def _():
        e_ref[0:POOL_HALO, :] = jnp.zeros((POOL_HALO, pw), F32)

    @pl.when(s > 0)
    def _():
        e_ref[0:POOL_HALO, :] = e_ref[tm:tm + POOL_HALO, :]

    def normed(r):
        x = x_ref[0, r * sub:(r + 1) * sub, :]
        h = _rms(x, nrm_ref[...]) * (1.0 + mod_ref[0, 1:2, :]) + mod_ref[0, 0:1, :]
        return h.astype(BF16)

    def project(r, hb):
        rows = slice(r * sub, (r + 1) * sub)
        mixer_in = jnp.dot(hb, w_in_ref[:, c0:], preferred_element_type=F32)
        for t, ref in enumerate((q_ref, k_ref, v_ref)):
            y = jnp.dot(hb, w_in_ref[:, t * sbw:(t + 1) * sbw], preferred_element_type=F32)
            for j in range(sbw // LANES):
                grp = y[:, j * LANES:(j + 1) * LANES]
                for hh in range(heads_per_group):
                    own = (lane_g >= hh * HEAD_DIM) & (lane_g < (hh + 1) * HEAD_DIM)
                    fill = jnp.where(lane_g == _spare_lane(hh), 1.0, 0.0) if t == 0 else 0.0
                    ref[0, heads_per_group * j + hh, rows, :] = jnp.where(own, grp, fill).astype(BF16)
        return mixer_in

    def pool(r, p):
        e_ref[POOL_HALO + r * sub:POOL_HALO + (r + 1) * sub, :] = p
        e = e_ref[r * sub:(r + 1) * sub + POOL_HALO, :]
        s2 = e + pltpu.roll(e, 1, 0)
        s4 = s2 + pltpu.roll(s2, 2, 0)
        s8 = s4 + pltpu.roll(s4, 4, 0)
        s16 = s8 + pltpu.roll(s8, 8, 0)
        wsum = jnp.where(lane_e < gdim, s2,
                         jnp.where(lane_e < 2 * gdim, s4, jnp.where(lane_e < 3 * gdim, s8, s16)))
        wsum = wsum[POOL_HALO:, :]
        count = jnp.minimum(s * tm + r * sub + row + 1, wlen).astype(F32)
        pooled = wsum / count - p
        mixed = jnp.dot(pooled.astype(BF16), wpool_ref[...], preferred_element_type=F32) * pscale_ref[...]
        pg_ref[0, r * sub:(r + 1) * sub, 0:pw] = _rms(mixed, mnp_ref[...]).astype(BF16)

    def gate(r, u, vg):
        ug = jax.nn.gelu(u)
        vge = jax.nn.gelu(vg)
        mu = jnp.mean(vge, axis=-1, keepdims=True)
        var = jnp.mean(jnp.square(vge - mu), axis=-1, keepdims=True)
        vn = ((vge - mu) * lax.rsqrt(var + EPS) * sgn_ref[...]).astype(BF16)
        zero = jnp.zeros((CHUNK, pw), BF16)
        outs = []
        for c in range(sub // CHUNK):
            vc = vn[c * CHUNK:(c + 1) * CHUNK, :]
            stacked = jnp.concatenate(
                [jnp.where((lane_c >= hh * hd) & (lane_c < (hh + 1) * hd), vc, zero) for hh in range(SG_HEADS)],
                axis=0)
            outs.append(jnp.dot(wcat_ref[...], stacked, preferred_element_type=F32) + sbias_ref[...])
        g_out = ug * jnp.concatenate(outs, axis=0)
        pg_ref[0, r * sub:(r + 1) * sub, pw:2 * pw] = _rms(g_out, mng_ref[...]).astype(BF16)

    n_sub = tm // sub
    hbs = [normed(r) for r in range(n_sub)]
    for r in range(n_sub):
        mixer_in = project(r, hbs[r])
        pool(r, mixer_in[:, :pw])
        gate(r, mixer_in[:, pw:2 * pw], mixer_in[:, 2 * pw:])


def _front(x, mod, nrm, w_in_b, wpool_bd, pscale, sgn, wcat, sbias, mnp, mng, *, layer, tm, sbw, pw):
    b, s, d = x.shape
    nh = sbw // HEAD_DIM
    in_w = w_in_b.shape[2]
    const = lambda *shape: pl.BlockSpec(shape, lambda bi, si: (0,) * len(shape))
    stacked = lambda *shape: pl.BlockSpec((None,) + shape, lambda bi, si: (layer,) + (0,) * len(shape))
    qkv_spec = pl.BlockSpec((1, nh, tm, LANES), lambda bi, si: (bi, 0, si, 0))
    qkv_shape = jax.ShapeDtypeStruct((b, nh, s, LANES), BF16)
    return pl.pallas_call(
        functools.partial(_front_kernel, tm=tm, sbw=sbw, pw=pw),
        grid=(b, s // tm),
        in_specs=[
            pl.BlockSpec((1, tm, d), lambda bi, si: (bi, si, 0)),
            pl.BlockSpec((1, N_MOD, d), lambda bi, si: (bi, 0, 0)),
            const(1, d), stacked(d, in_w), const(pw, pw), const(1, pw), const(1, pw),
            const(CHUNK, SG_HEADS * CHUNK), const(CHUNK, pw), const(1, pw), const(1, pw),
        ],
        out_specs=[qkv_spec, qkv_spec, qkv_spec,
                   pl.BlockSpec((1, tm, 2 * pw), lambda bi, si: (bi, si, 0))],
        out_shape=[qkv_shape, qkv_shape, qkv_shape, jax.ShapeDtypeStruct((b, s, 2 * pw), BF16)],
        scratch_shapes=[pltpu.VMEM((tm + POOL_HALO, pw), F32)],
        compiler_params=pltpu.CompilerParams(
            dimension_semantics=("arbitrary", "arbitrary"), vmem_limit_bytes=VMEM_LIMIT),
        name="front",
    )(x, mod, nrm, w_in_b, wpool_bd, pscale, sgn, wcat, sbias, mnp, mng)


TRI_WIDTH = 256
SCORE_LEAD = 3


def _neg_abs(z):
    sign = jnp.uint32(0x80000000)
    return lax.bitcast_convert_type(lax.bitcast_convert_type(z, jnp.uint32) | sign, F32)


def _mixer_ffn_kernel(q_ref, kp_ref, kc_ref, vp_ref, vc_ref, pg_ref, x_ref, moda_ref, modf_ref, lc_ref, tri_ref,
                      mna_ref, wout_ref, nrm_ref, wg_ref, wu_ref, wd_ref, fn_ref, o_ref, x1_scr, res_scr,
                      *, tq, nh, n_seq, n_tiles, final):
    j = pl.program_id(0)
    i = jnp.minimum(j, n_tiles - 1) % n_seq
    nqb = tq // Q_BLOCK
    edge = KV_SPAN - Q_BLOCK
    group = LANES // HEAD_DIM
    assert KV_SPAN - 2 * Q_BLOCK + Q_BLOCK == 2 * TRI_WIDTH

    @pl.when(j == 0)
    def _():
        x1_scr[...] = jnp.zeros(x1_scr.shape, F32)

    res_scr[...] = x1_scr[...]

    sub = FFN_SUB
    subs = [slice(r * sub, (r + 1) * sub) for r in range(tq // sub)]
    dff = wg_ref.shape[1]
    d_model = wd_ref.shape[1]
    n_ct = dff // FFN_COLS
    hbs, acts = {}, {}

    def ffn_norm(r):
        h = _rms(res_scr[subs[r], :], nrm_ref[...]) * (1.0 + modf_ref[0, 4:5, :]) + modf_ref[0, 3:4, :]
        hbs[r] = h.astype(BF16)

    def ffn_gate_up(r, c):
        cols = slice(c * FFN_COLS, (c + 1) * FFN_COLS)
        gate = jnp.dot(hbs[r], wg_ref[:, cols], preferred_element_type=F32)
        up = jnp.dot(hbs[r], wu_ref[:, cols], preferred_element_type=F32)
        acts[r, c] = (gate / (1.0 + jnp.exp(-gate)) * up).astype(BF16)

    def ffn_down(r, c):
        cols = slice(c * FFN_COLS, (c + 1) * FFN_COLS)
        act = jnp.concatenate([acts[r, k] for k in range(n_ct)], axis=1)
        y = jnp.dot(act, wd_ref[:, cols], preferred_element_type=F32)
        out = res_scr[subs[r], cols] + modf_ref[0, 5:6, cols] * y
        o_ref[0, subs[r], cols] = out

    def ffn_finish(r):
        if final:
            o_ref[0, subs[r], :] = _rms(o_ref[0, subs[r], :], fn_ref[...])

    pieces = []
    for r in range(len(subs)):
        pieces += [functools.partial(ffn_gate_up, r, c) for c in range(n_ct)]
        pieces += [functools.partial(ffn_down, r, c) for c in range(d_model // FFN_COLS)]
        pieces.append(functools.partial(ffn_finish, r))

    kb_prev = jnp.where(i == 0, NEG_BIG, 0.0).astype(BF16)
    prev_bias = [kb_prev * lc_ref[hh:hh + 1, :] for hh in range(group)]

    def window(prev_ref, cur_ref, h, off):
        return prev_ref[0, h, off:, :], cur_ref[0, h, 0:off + Q_BLOCK, :]

    newest = (lax.broadcasted_iota(jnp.int32, (Q_BLOCK, Q_BLOCK), 1)
              < lax.broadcasted_iota(jnp.int32, (Q_BLOCK, Q_BLOCK), 0))
    half = TRI_WIDTH - Q_BLOCK

    def scores(item):
        qb, h = divmod(item, nh)
        off = qb * Q_BLOCK
        k_prev, k_cur = window(kp_ref, kc_ref, h, off)
        keys = jnp.concatenate([k_prev + prev_bias[h % group], k_cur], axis=0)
        z = lax.dot_general(q_ref[0, h, off:off + Q_BLOCK, :], keys, (((1,), (1,)), ((), ())),
                            preferred_element_type=F32)
        return jnp.concatenate([jnp.where(newest, z[:, edge:], z[:, :Q_BLOCK]), z[:, Q_BLOCK:edge]], axis=1)

    def softplus(z):
        n = jnp.maximum(z, 0.0) + jnp.log(1.0 + jnp.exp2(_neg_abs(z))) * LOG2E
        nb = n.astype(BF16)
        n_new = jnp.where(newest, n[:, :Q_BLOCK], 0.0)
        nb_new = n_new.astype(BF16)
        nb_old = nb[:, :Q_BLOCK] - nb_new
        tot_new = jnp.sum(n_new, axis=-1, keepdims=True)
        tot_mid = jnp.sum(n[:, Q_BLOCK + half:], axis=-1, keepdims=True)
        chunks = (jnp.concatenate([nb_old, nb[:, Q_BLOCK:Q_BLOCK + half]], axis=1), nb[:, Q_BLOCK + half:], nb_new)
        return z - n, chunks, (tot_mid + tot_new, tot_new)

    def log_weight(state):
        dmn, (c_first, c_mid, c_new), (tot_first, tot_mid) = state
        l_first = jnp.dot(c_first, tri_ref[...], preferred_element_type=F32) + tot_first
        l_mid = jnp.dot(c_mid, tri_ref[...], preferred_element_type=F32) + tot_mid
        l_new = jnp.dot(c_new, tri_ref[0:Q_BLOCK, 0:Q_BLOCK], preferred_element_type=F32)
        later = jnp.concatenate([jnp.where(newest, l_new, l_first[:, :Q_BLOCK]), l_first[:, Q_BLOCK:], l_mid],
                                axis=1)
        return dmn - later

    def weights(arg):
        w = jnp.exp2(arg)
        wb = w.astype(BF16)
        w_new = jnp.where(newest, w[:, :Q_BLOCK], 0.0).astype(BF16)
        return jnp.concatenate([wb[:, :Q_BLOCK] - w_new, wb[:, Q_BLOCK:], w_new], axis=1)

    def weighted_values(item, w):
        qb, h = divmod(item, nh)
        values = jnp.concatenate(window(vp_ref, vc_ref, h, qb * Q_BLOCK), axis=0)
        return jnp.dot(w, values, preferred_element_type=F32)

    def project(qb, pvs):
        a = jnp.concatenate([sum(pvs[g * group + 1:(g + 1) * group], pvs[g * group])
                             for g in range(nh // group)], axis=1)
        rows = slice(qb * Q_BLOCK, (qb + 1) * Q_BLOCK)
        an = _rms(a, mna_ref[...]).astype(BF16)
        merged = jnp.concatenate([an, pg_ref[0, rows, :]], axis=1)
        y = jnp.dot(merged, wout_ref[...], preferred_element_type=F32)
        x1_scr[rows, :] = x_ref[0, rows, :] + moda_ref[0, 2:3, :] * y

    n_items = nqb * nh
    per_step = -(-len(pieces) // n_items)
    zs, states, ws, pvs = {}, {}, {}, {}
    for step in range(-SCORE_LEAD, n_items + 2):
        if step == 0:
            for r in range(len(subs)):
                ffn_norm(r)
        for _ in range(per_step):
            if pieces and step >= 0:
                pieces.pop(0)()
        if 0 <= step + SCORE_LEAD < n_items:
            zs[step + SCORE_LEAD] = scores(step + SCORE_LEAD)
        if 0 <= step < n_items:
            states[step] = softplus(zs.pop(step))
        if 0 <= step - 1 < n_items:
            ws[step - 1] = weights(log_weight(states.pop(step - 1)))
        if 0 <= step - 2 < n_items:
            item = step - 2
            pvs[item] = weighted_values(item, ws.pop(item))
            if item % nh == nh - 1:
                qb = item // nh
                project(qb, [pvs.pop(qb * nh + h) for h in range(nh)])
    assert not pieces


def _mixer_ffn(q, k, v, pg, x, mod, lane_consts, tri, mna, wout_b, nrm, wgu_b, wd_b, fn, *, layer, tq, final):
    b, s, d = x.shape
    nh = q.shape[1]
    mixw = wout_b.shape[1]
    dff = wd_b.shape[1]
    n_seq = s // tq
    n_tiles = b * n_seq
    once = lambda *shape: pl.BlockSpec(shape, lambda j: (0,) * len(shape), pipeline_mode=pl.Buffered(1))
    stacked = lambda *shape, col=0: pl.BlockSpec((None,) + shape, lambda j: (layer, 0, col),
                                                 pipeline_mode=pl.Buffered(1))
    mixer_tile = lambda j: jnp.minimum(j, n_tiles - 1)
    ffn_tile = lambda j: jnp.maximum(j - 1, 0)
    cur = pl.BlockSpec((1, nh, tq, LANES), lambda j: (mixer_tile(j) // n_seq, 0, mixer_tile(j) % n_seq, 0))
    prev = pl.BlockSpec((1, nh, tq, LANES),
                        lambda j: (mixer_tile(j) // n_seq, 0, jnp.maximum(mixer_tile(j) % n_seq - 1, 0), 0))
    rows_a = lambda width: pl.BlockSpec((1, tq, width),
                                        lambda j: (mixer_tile(j) // n_seq, mixer_tile(j) % n_seq, 0))
    return pl.pallas_call(
        functools.partial(_mixer_ffn_kernel, tq=tq, nh=nh, n_seq=n_seq, n_tiles=n_tiles, final=final),
        grid=(n_tiles + 1,),
        in_specs=[
            cur, prev, cur, prev, cur, rows_a(pg.shape[2]), rows_a(d),
            pl.BlockSpec((1, N_MOD, d), lambda j: (mixer_tile(j) // n_seq, 0, 0)),
            pl.BlockSpec((1, N_MOD, d), lambda j: (ffn_tile(j) // n_seq, 0, 0)),
            once(*lane_consts.shape), once(*tri.shape), once(1, nh * HEAD_DIM), stacked(mixw, d),
            once(1, d), stacked(d, dff, col=0), stacked(d, dff, col=1), stacked(dff, d), once(1, d),
        ],
        out_specs=pl.BlockSpec((1, tq, d), lambda j: (ffn_tile(j) // n_seq, ffn_tile(j) % n_seq, 0)),
        out_shape=jax.ShapeDtypeStruct((b, s, d), F32),
        scratch_shapes=[pltpu.VMEM((tq, d), F32), pltpu.VMEM((tq, d), F32)],
        compiler_params=pltpu.CompilerParams(dimension_semantics=("arbitrary",), vmem_limit_bytes=VMEM_LIMIT),
        name="mixer_ffn",
    )(q, k, k, v, v, pg, x, mod, mod, lane_consts, tri, mna, wout_b, nrm, wgu_b, wgu_b, wd_b, fn)


def _lane_constants():
    rows = np.zeros((16, LANES), np.float32)
    for hh in range(LANES // HEAD_DIM):
        rows[hh, _spare_lane(hh)] = 1.0
    return rows


def _suffix_sum_matrix(n):
    j = np.arange(n)
    return (j[:, None] > j[None, :]).astype(np.float32)


def kernel(x, c, w_ada, b_ada, norm_mix_in, w_in, w_pool, pool_scale, sg_norm, w_s, b_s, mix_norm, w_out,
           norm_ffn_in, w_gate_up, w_down, final_norm):
    b, s, d = x.shape
    depth = w_ada.shape[0]
    pw = pool_scale.shape[1]
    sbw = (w_in.shape[2] - 3 * pw) // 3
    tm = min(ROW_TILE, s)
    tf = min(FRONT_TILE, s)
    assert s % tm == 0 and tm % Q_BLOCK == 0 and tm >= SB_WINDOW and sbw % LANES == 0
    assert s % tf == 0 and tf % FRONT_SUB == 0 and FRONT_SUB % CHUNK == 0
    assert tm % FFN_SUB == 0

    mod_all = _ada(c, w_ada, b_ada).reshape(depth, b, N_MOD, d)

    q_scale = jnp.concatenate([jnp.full((sbw,), LOG2E * HEAD_DIM ** -0.5, F32),
                               jnp.ones((w_in.shape[2] - sbw,), F32)])
    w_in_b = (w_in * q_scale).astype(BF16)
    groups = w_pool.shape[1]
    gdim = w_pool.shape[2]
    eye = jnp.eye(groups, dtype=F32)
    wpool_bd = (w_pool[:, :, :, None, :] * eye[None, :, None, :, None]).reshape(
        depth, groups * gdim, groups * gdim).astype(BF16)
    causal = jnp.tril(jnp.ones((CHUNK, CHUNK), dtype=bool))
    ws_c = jnp.where(causal[None, None], w_s, 0.0)
    wcat = ws_c.transpose(0, 2, 1, 3).reshape(depth, CHUNK, SG_HEADS * CHUNK).astype(BF16)
    sbias = jnp.repeat(b_s.transpose(0, 2, 1), pw // SG_HEADS, axis=2)
    w_out_b = w_out.astype(BF16)
    wgu_b = w_gate_up.astype(BF16)
    wd_b = w_down.astype(BF16)
    lane_consts = jnp.asarray(_lane_constants(), dtype=BF16)
    tri = jnp.asarray(_suffix_sum_matrix(TRI_WIDTH), dtype=BF16)
    row = lambda v: v.reshape(1, -1)

    for l in range(depth):
        mod = mod_all[l]
        q, k, v, pg = _front(x, mod, row(norm_mix_in[l]), w_in_b, wpool_bd[l], row(pool_scale[l]),
                             row(sg_norm[l]), wcat[l], sbias[l], row(mix_norm[l, sbw:sbw + pw]),
                             row(mix_norm[l, sbw + pw:]), layer=l, tm=tf, sbw=sbw, pw=pw)
        x = _mixer_ffn(q, k, v, pg, x, mod, lane_consts, tri, row(mix_norm[l, :sbw]), w_out_b,
                       row(norm_ffn_in[l]), wgu_b, wd_b, row(final_norm),
                       layer=l, tq=tm, final=(l == depth - 1))
    return x
```

```python
import functools

import numpy as np
import jax
import jax.numpy as jnp
from jax import lax
from jax.experimental import pallas as pl
from jax.experimental.pallas import tpu as pltpu

F32 = jnp.float32
BF16 = jnp.bfloat16

HEAD_DIM = 64
Q_BLOCK = 128
SB_WINDOW = 512
KV_SPAN = SB_WINDOW + Q_BLOCK
POOL_WINDOWS = (2, 4, 8, 16)
POOL_HALO = 16
CHUNK = 128
SG_HEADS = 4
N_MOD = 6
EPS = 1e-6
LANES = 128
NEG_BIG = -1e30
LOG2E = 1.4426950408889634

ROW_TILE = 512
FRONT_TILE = 1024
FRONT_SUB = 256
FFN_SUB = 256
FFN_COLS = 256
VMEM_LIMIT = 56 * 1024 * 1024


def _spare_lane(head_in_group):
    return HEAD_DIM if head_in_group == 0 else 0


def _rms(x, gain):
    return x * lax.rsqrt(jnp.mean(x * x, axis=-1, keepdims=True) + EPS) * gain


def _ada_kernel(c_ref, w_ref, b_ref, o_ref):
    c = c_ref[...]
    c_act = c / (1.0 + jnp.exp(-c))
    o_ref[0] = jnp.dot(c_act, w_ref[0], preferred_element_type=F32,
                       precision=lax.Precision.HIGHEST) + b_ref[0]


def _ada(c, w_ada, b_ada):
    depth, d, nd = w_ada.shape
    b = c.shape[0]
    return pl.pallas_call(
        _ada_kernel,
        grid=(depth, nd // d),
        in_specs=[
            pl.BlockSpec((b, d), lambda l, j: (0, 0)),
            pl.BlockSpec((1, d, d), lambda l, j: (l, 0, j)),
            pl.BlockSpec((1, 1, d), lambda l, j: (l, 0, j)),
        ],
        out_specs=pl.BlockSpec((1, b, d), lambda l, j: (l, 0, j)),
        out_shape=jax.ShapeDtypeStruct((depth, b, nd), F32),
        name="ada_mod",
    )(c, w_ada, b_ada.reshape(depth, 1, nd))


def _front_kernel(x_ref, mod_ref, nrm_ref, w_in_ref, wpool_ref, pscale_ref, sgn_ref, wcat_ref,
                  sbias_ref, mnp_ref, mng_ref, q_ref, k_ref, v_ref, pg_ref, e_ref, *, tm, sbw, pw):
    s = pl.program_id(1)
    sub = FRONT_SUB
    c0 = 3 * sbw
    heads_per_group = LANES // HEAD_DIM
    gdim = pw // len(POOL_WINDOWS)
    hd = pw // SG_HEADS
    lane_g = lax.broadcasted_iota(jnp.int32, (sub, LANES), 1)
    lane_e = lax.broadcasted_iota(jnp.int32, (sub + POOL_HALO, pw), 1)
    lane = lax.broadcasted_iota(jnp.int32, (sub, pw), 1)
    row = lax.broadcasted_iota(jnp.int32, (sub, pw), 0)
    lane_c = lax.broadcasted_iota(jnp.int32, (CHUNK, pw), 1)
    wlen = jnp.where(lane < gdim, POOL_WINDOWS[0],
                     jnp.where(lane < 2 * gdim, POOL_WINDOWS[1],
                               jnp.where(lane < 3 * gdim, POOL_WINDOWS[2], POOL_WINDOWS[3])))

    @pl.when(s == 0)
    def _():
        e_ref[0:POOL_HALO, :] = jnp.zeros((POOL_HALO, pw), F32)

    @pl.when(s > 0)
    def _():
        e_ref[0:POOL_HALO, :] = e_ref[tm:tm + POOL_HALO, :]

    def normed(r):
        x = x_ref[0, r * sub:(r + 1) * sub, :]
        h = _rms(x, nrm_ref[...]) * (1.0 + mod_ref[0, 1:2, :]) + mod_ref[0, 0:1, :]
        return h.astype(BF16)

    def project(r, hb):
        rows = slice(r * sub, (r + 1) * sub)
        mixer_in = jnp.dot(hb, w_in_ref[:, c0:], preferred_element_type=F32)
        for t, ref in enumerate((q_ref, k_ref, v_ref)):
            y = jnp.dot(hb, w_in_ref[:, t * sbw:(t + 1) * sbw], preferred_element_type=F32)
            for j in range(sbw // LANES):
                grp = y[:, j * LANES:(j + 1) * LANES]
                for hh in range(heads_per_group):
                    own = (lane_g >= hh * HEAD_DIM) & (lane_g < (hh + 1) * HEAD_DIM)
                    fill = jnp.where(lane_g == _spare_lane(hh), 1.0, 0.0) if t == 0 else 0.0
                    ref[0, heads_per_group * j + hh, rows, :] = jnp.where(own, grp, fill).astype(BF16)
        return mixer_in

    def pool(r, p):
        e_ref[POOL_HALO + r * sub:POOL_HALO + (r + 1) * sub, :] = p
        e = e_ref[r * sub:(r + 1) * sub + POOL_HALO, :]
        s2 = e + pltpu.roll(e, 1, 0)
        s4 = s2 + pltpu.roll(s2, 2, 0)
        s8 = s4 + pltpu.roll(s4, 4, 0)
        s16 = s8 + pltpu.roll(s8, 8, 0)
        wsum = jnp.where(lane_e < gdim, s2,
                         jnp.where(lane_e < 2 * gdim, s4, jnp.where(lane_e < 3 * gdim, s8, s16)))
        wsum = wsum[POOL_HALO:, :]
        count = jnp.minimum(s * tm + r * sub + row + 1, wlen).astype(F32)
        pooled = wsum / count - p
        mixed = jnp.dot(pooled.astype(BF16), wpool_ref[...], preferred_element_type=F32) * pscale_ref[...]
        pg_ref[0, r * sub:(r + 1) * sub, 0:pw] = _rms(mixed, mnp_ref[...]).astype(BF16)

    def gate(r, u, vg):
        ug = jax.nn.gelu(u)
        vge = jax.nn.gelu(vg)
        mu = jnp.mean(vge, axis=-1, keepdims=True)
        var = jnp.mean(jnp.square(vge - mu), axis=-1, keepdims=True)
        vn = ((vge - mu) * lax.rsqrt(var + EPS) * sgn_ref[...]).astype(BF16)
        zero = jnp.zeros((CHUNK, pw), BF16)
        outs = []
        for c in range(sub // CHUNK):
            vc = vn[c * CHUNK:(c + 1) * CHUNK, :]
            stacked = jnp.concatenate(
                [jnp.where((lane_c >= hh * hd) & (lane_c < (hh + 1) * hd), vc, zero) for hh in range(SG_HEADS)],
                axis=0)
            outs.append(jnp.dot(wcat_ref[...], stacked, preferred_element_type=F32) + sbias_ref[...])
        g_out = ug * jnp.concatenate(outs, axis=0)
        pg_ref[0, r * sub:(r + 1) * sub, pw:2 * pw] = _rms(g_out, mng_ref[...]).astype(BF16)

    n_sub = tm // sub
    hbs = [normed(r) for r in range(n_sub)]
    for r in range(n_sub):
        mixer_in = project(r, hbs[r])
        pool(r, mixer_in[:, :pw])
        gate(r, mixer_in[:, pw:2 * pw], mixer_in[:, 2 * pw:])


def _front(x, mod, nrm, w_in_b, wpool_bd, pscale, sgn, wcat, sbias, mnp, mng, *, layer, tm, sbw, pw):
    b, s, d = x.shape
    nh = sbw // HEAD_DIM
    in_w = w_in_b.shape[2]
    const = lambda *shape: pl.BlockSpec(shape, lambda bi, si: (0,) * len(shape))
    stacked = lambda *shape: pl.BlockSpec((None,) + shape, lambda bi, si: (layer,) + (0,) * len(shape))
    qkv_spec = pl.BlockSpec((1, nh, tm, LANES), lambda bi, si: (bi, 0, si, 0))
    qkv_shape = jax.ShapeDtypeStruct((b, nh, s, LANES), BF16)
    return pl.pallas_call(
        functools.partial(_front_kernel, tm=tm, sbw=sbw, pw=pw),
        grid=(b, s // tm),
        in_specs=[
            pl.BlockSpec((1, tm, d), lambda bi, si: (bi, si, 0)),
            pl.BlockSpec((1, N_MOD, d), lambda bi, si: (bi, 0, 0)),
            const(1, d), stacked(d, in_w), const(pw, pw), const(1, pw), const(1, pw),
            const(CHUNK, SG_HEADS * CHUNK), const(CHUNK, pw), const(1, pw), const(1, pw),
        ],
        out_specs=[qkv_spec, qkv_spec, qkv_spec,
                   pl.BlockSpec((1, tm, 2 * pw), lambda bi, si: (bi, si, 0))],
        out_shape=[qkv_shape, qkv_shape, qkv_shape, jax.ShapeDtypeStruct((b, s, 2 * pw), BF16)],
        scratch_shapes=[pltpu.VMEM((tm + POOL_HALO, pw), F32)],
        compiler_params=pltpu.CompilerParams(
            dimension_semantics=("arbitrary", "arbitrary"), vmem_limit_bytes=VMEM_LIMIT),
        name="front",
    )(x, mod, nrm, w_in_b, wpool_bd, pscale, sgn, wcat, sbias, mnp, mng)


TRI_WIDTH = 256


def _neg_abs(z):
    sign = jnp.uint32(0x80000000)
    return lax.bitcast_convert_type(lax.bitcast_convert_type(z, jnp.uint32) | sign, F32)


def _mixer_ffn_kernel(q_ref, kp_ref, kc_ref, vp_ref, vc_ref, pg_ref, x_ref, moda_ref, modf_ref, lc_ref, tri_ref,
                      mna_ref, wout_ref, nrm_ref, wg_ref, wu_ref, wd_ref, fn_ref, o_ref, x1_scr, res_scr,
                      *, tq, nh, n_seq, n_tiles, final):
    j = pl.program_id(0)
    i = jnp.minimum(j, n_tiles - 1) % n_seq
    nqb = tq // Q_BLOCK
    edge = KV_SPAN - Q_BLOCK
    group = LANES // HEAD_DIM
    assert KV_SPAN - 2 * Q_BLOCK + Q_BLOCK == 2 * TRI_WIDTH

    @pl.when(j == 0)
    def _():
        x1_scr[...] = jnp.zeros(x1_scr.shape, F32)

    res_scr[...] = x1_scr[...]

    sub = FFN_SUB
    subs = [slice(r * sub, (r + 1) * sub) for r in range(tq // sub)]
    dff = wg_ref.shape[1]
    d_model = wd_ref.shape[1]
    n_ct = dff // FFN_COLS
    hbs, acts = {}, {}

    def ffn_norm(r):
        h = _rms(res_scr[subs[r], :], nrm_ref[...]) * (1.0 + modf_ref[0, 4:5, :]) + modf_ref[0, 3:4, :]
        hbs[r] = h.astype(BF16)

    def ffn_gate_up(r, c):
        cols = slice(c * FFN_COLS, (c + 1) * FFN_COLS)
        gate = jnp.dot(hbs[r], wg_ref[:, cols], preferred_element_type=F32)
        up = jnp.dot(hbs[r], wu_ref[:, cols], preferred_element_type=F32)
        acts[r, c] = (gate / (1.0 + jnp.exp(-gate)) * up).astype(BF16)

    def ffn_down(r, c):
        cols = slice(c * FFN_COLS, (c + 1) * FFN_COLS)
        act = jnp.concatenate([acts[r, k] for k in range(n_ct)], axis=1)
        y = jnp.dot(act, wd_ref[:, cols], preferred_element_type=F32)
        out = res_scr[subs[r], cols] + modf_ref[0, 5:6, cols] * y
        o_ref[0, subs[r], cols] = out

    def ffn_finish(r):
        if final:
            o_ref[0, subs[r], :] = _rms(o_ref[0, subs[r], :], fn_ref[...])

    pieces = []
    for r in range(len(subs)):
        pieces += [functools.partial(ffn_gate_up, r, c) for c in range(n_ct)]
        pieces += [functools.partial(ffn_down, r, c) for c in range(d_model // FFN_COLS)]
        pieces.append(functools.partial(ffn_finish, r))

    kb_prev = jnp.where(i == 0, NEG_BIG, 0.0).astype(BF16)
    prev_bias = [kb_prev * lc_ref[hh:hh + 1, :] for hh in range(group)]

    def window(prev_ref, cur_ref, h, off):
        return prev_ref[0, h, off:, :], cur_ref[0, h, 0:off + Q_BLOCK, :]

    newest = (lax.broadcasted_iota(jnp.int32, (Q_BLOCK, Q_BLOCK), 1)
              < lax.broadcasted_iota(jnp.int32, (Q_BLOCK, Q_BLOCK), 0))
    half = TRI_WIDTH - Q_BLOCK

    def scores(item):
        qb, h = divmod(item, nh)
        off = qb * Q_BLOCK
        k_prev, k_cur = window(kp_ref, kc_ref, h, off)
        keys = jnp.concatenate([k_prev + prev_bias[h % group], k_cur], axis=0)
        z = lax.dot_general(q_ref[0, h, off:off + Q_BLOCK, :], keys, (((1,), (1,)), ((), ())),
                            preferred_element_type=F32)
        return jnp.concatenate([jnp.where(newest, z[:, edge:], z[:, :Q_BLOCK]), z[:, Q_BLOCK:edge]], axis=1)

    def softplus(z):
        n = jnp.maximum(z, 0.0) + jnp.log(1.0 + jnp.exp2(_neg_abs(z))) * LOG2E
        nb = n.astype(BF16)
        n_new = jnp.where(newest, n[:, :Q_BLOCK], 0.0)
        nb_new = n_new.astype(BF16)
        nb_old = nb[:, :Q_BLOCK] - nb_new
        tot_new = jnp.sum(n_new, axis=-1, keepdims=True)
        tot_mid = jnp.sum(n[:, Q_BLOCK + half:], axis=-1, keepdims=True)
        chunks = (jnp.concatenate([nb_old, nb[:, Q_BLOCK:Q_BLOCK + half]], axis=1), nb[:, Q_BLOCK + half:], nb_new)
        return z - n, chunks, (tot_mid + tot_new, tot_new)

    def log_weight(state):
        dmn, (c_first, c_mid, c_new), (tot_first, tot_mid) = state
        l_first = jnp.dot(c_first, tri_ref[...], preferred_element_type=F32) + tot_first
        l_mid = jnp.dot(c_mid, tri_ref[...], preferred_element_type=F32) + tot_mid
        l_new = jnp.dot(c_new, tri_ref[0:Q_BLOCK, 0:Q_BLOCK], preferred_element_type=F32)
        later = jnp.concatenate([jnp.where(newest, l_new, l_first[:, :Q_BLOCK]), l_first[:, Q_BLOCK:], l_mid],
                                axis=1)
        return dmn - later

    def weights(arg):
        w = jnp.exp2(arg)
        wb = w.astype(BF16)
        w_new = jnp.where(newest, w[:, :Q_BLOCK], 0.0).astype(BF16)
        return jnp.concatenate([wb[:, :Q_BLOCK] - w_new, wb[:, Q_BLOCK:], w_new], axis=1)

    def weighted_values(item, w):
        qb, h = divmod(item, nh)
        values = jnp.concatenate(window(vp_ref, vc_ref, h, qb * Q_BLOCK), axis=0)
        return jnp.dot(w, values, preferred_element_type=F32)

    def project(qb, pvs):
        a = jnp.concatenate([sum(pvs[g * group + 1:(g + 1) * group], pvs[g * group])
                             for g in range(nh // group)], axis=1)
        rows = slice(qb * Q_BLOCK, (qb + 1) * Q_BLOCK)
        an = _rms(a, mna_ref[...]).astype(BF16)
        merged = jnp.concatenate([an, pg_ref[0, rows, :]], axis=1)
        y = jnp.dot(merged, wout_ref[...], preferred_element_type=F32)
        x1_scr[rows, :] = x_ref[0, rows, :] + moda_ref[0, 2:3, :] * y

    n_items = nqb * nh
    per_step = -(-len(pieces) // n_items)
    zs, states, ws, pvs = {}, {}, {}, {}
    for step in range(-2, n_items + 2):
        if step == 0:
            for r in range(len(subs)):
                ffn_norm(r)
        if 0 <= step + 2 < n_items:
            zs[step + 2] = scores(step + 2)
        if 0 <= step - 1 < n_items:
            ws[step - 1] = weights(log_weight(states.pop(step - 1)))
        for _ in range(per_step):
            if pieces and step >= 0:
                pieces.pop(0)()
        if 0 <= step < n_items:
            states[step] = softplus(zs.pop(step))
        if 0 <= step - 2 < n_items:
            item = step - 2
            pvs[item] = weighted_values(item, ws.pop(item))
            if item % nh == nh - 1:
                qb = item // nh
                project(qb, [pvs.pop(qb * nh + h) for h in range(nh)])
    assert not pieces


def _mixer_ffn(q, k, v, pg, x, mod, lane_consts, tri, mna, wout_b, nrm, wgu_b, wd_b, fn, *, layer, tq, final):
    b, s, d = x.shape
    nh = q.shape[1]
    mixw = wout_b.shape[1]
    dff = wd_b.shape[1]
    n_seq = s // tq
    n_tiles = b * n_seq
    once = lambda *shape: pl.BlockSpec(shape, lambda j: (0,) * len(shape), pipeline_mode=pl.Buffered(1))
    stacked = lambda *shape, col=0: pl.BlockSpec((None,) + shape, lambda j: (layer, 0, col),
                                                 pipeline_mode=pl.Buffered(1))
    mixer_tile = lambda j: jnp.minimum(j, n_tiles - 1)
    ffn_tile = lambda j: jnp.maximum(j - 1, 0)
    cur = pl.BlockSpec((1, nh, tq, LANES), lambda j: (mixer_tile(j) // n_seq, 0, mixer_tile(j) % n_seq, 0))
    prev = pl.BlockSpec((1, nh, tq, LANES),
                        lambda j: (mixer_tile(j) // n_seq, 0, jnp.maximum(mixer_tile(j) % n_seq - 1, 0), 0))
    rows_a = lambda width: pl.BlockSpec((1, tq, width),
                                        lambda j: (mixer_tile(j) // n_seq, mixer_tile(j) % n_seq, 0))
    return pl.pallas_call(
        functools.partial(_mixer_ffn_kernel, tq=tq, nh=nh, n_seq=n_seq, n_tiles=n_tiles, final=final),
        grid=(n_tiles + 1,),
        in_specs=[
            cur, prev, cur, prev, cur, rows_a(pg.shape[2]), rows_a(d),
            pl.BlockSpec((1, N_MOD, d), lambda j: (mixer_tile(j) // n_seq, 0, 0)),
            pl.BlockSpec((1, N_MOD, d), lambda j: (ffn_tile(j) // n_seq, 0, 0)),
            once(*lane_consts.shape), once(*tri.shape), once(1, nh * HEAD_DIM), stacked(mixw, d),
            once(1, d), stacked(d, dff, col=0), stacked(d, dff, col=1), stacked(dff, d), once(1, d),
        ],
        out_specs=pl.BlockSpec((1, tq, d), lambda j: (ffn_tile(j) // n_seq, ffn_tile(j) % n_seq, 0)),
        out_shape=jax.ShapeDtypeStruct((b, s, d), F32),
        scratch_shapes=[pltpu.VMEM((tq, d), F32), pltpu.VMEM((tq, d), F32)],
        compiler_params=pltpu.CompilerParams(dimension_semantics=("arbitrary",), vmem_limit_bytes=VMEM_LIMIT),
        name="mixer_ffn",
    )(q, k, k, v, v, pg, x, mod, mod, lane_consts, tri, mna, wout_b, nrm, wgu_b, wgu_b, wd_b, fn)


def _lane_constants():
    rows = np.zeros((16, LANES), np.float32)
    for hh in range(LANES // HEAD_DIM):
        rows[hh, _spare_lane(hh)] = 1.0
    return rows


def _suffix_sum_matrix(n):
    j = np.arange(n)
    return (j[:, None] > j[None, :]).astype(np.float32)


def kernel(x, c, w_ada, b_ada, norm_mix_in, w_in, w_pool, pool_scale, sg_norm, w_s, b_s, mix_norm, w_out,
           norm_ffn_in, w_gate_up, w_down, final_norm):
    b, s, d = x.shape
    depth = w_ada.shape[0]
    pw = pool_scale.shape[1]
    sbw = (w_in.shape[2] - 3 * pw) // 3
    tm = min(ROW_TILE, s)
    tf = min(FRONT_TILE, s)
    assert s % tm == 0 and tm % Q_BLOCK == 0 and tm >= SB_WINDOW and sbw % LANES == 0
    assert s % tf == 0 and tf % FRONT_SUB == 0 and FRONT_SUB % CHUNK == 0
    assert tm % FFN_SUB == 0

    mod_all = _ada(c, w_ada, b_ada).reshape(depth, b, N_MOD, d)

    q_scale = jnp.concatenate([jnp.full((sbw,), LOG2E * HEAD_DIM ** -0.5, F32),
                               jnp.ones((w_in.shape[2] - sbw,), F32)])
    w_in_b = (w_in * q_scale).astype(BF16)
    groups = w_pool.shape[1]
    gdim = w_pool.shape[2]
    eye = jnp.eye(groups, dtype=F32)
    wpool_bd = (w_pool[:, :, :, None, :] * eye[None, :, None, :, None]).reshape(
        depth, groups * gdim, groups * gdim).astype(BF16)
    causal = jnp.tril(jnp.ones((CHUNK, CHUNK), dtype=bool))
    ws_c = jnp.where(causal[None, None], w_s, 0.0)
    wcat = ws_c.transpose(0, 2, 1, 3).reshape(depth, CHUNK, SG_HEADS * CHUNK).astype(BF16)
    sbias = jnp.repeat(b_s.transpose(0, 2, 1), pw // SG_HEADS, axis=2)
    w_out_b = w_out.astype(BF16)
    wgu_b = w_gate_up.astype(BF16)
    wd_b = w_down.astype(BF16)
    lane_consts = jnp.asarray(_lane_constants(), dtype=BF16)
    tri = jnp.asarray(_suffix_sum_matrix(TRI_WIDTH), dtype=BF16)
    row = lambda v: v.reshape(1, -1)

    for l in range(depth):
        mod = mod_all[l]
        q, k, v, pg = _front(x, mod, row(norm_mix_in[l]), w_in_b, wpool_bd[l], row(pool_scale[l]),
                             row(sg_norm[l]), wcat[l], sbias[l], row(mix_norm[l, sbw:sbw + pw]),
                             row(mix_norm[l, sbw + pw:]), layer=l, tm=tf, sbw=sbw, pw=pw)
        x = _mixer_ffn(q, k, v, pg, x, mod, lane_consts, tri, row(mix_norm[l, :sbw]), w_out_b,
                       row(norm_ffn_in[l]), wgu_b, wd_b, row(final_norm),
                       layer=l, tq=tm, final=(l == depth - 1))
    return x
```

```python
import functools

import numpy as np
import jax
import jax.numpy as jnp
from jax import lax
from jax.experimental import pallas as pl
from jax.experimental.pallas import tpu as pltpu

F32 = jnp.float32
BF16 = jnp.bfloat16

HEAD_DIM = 64
Q_BLOCK = 128
SB_WINDOW = 512
KV_SPAN = SB_WINDOW + Q_BLOCK
POOL_WINDOWS = (2, 4, 8, 16)
POOL_HALO = 16
CHUNK = 128
SG_HEADS = 4
N_MOD = 6
EPS = 1e-6
LANES = 128
NEG_BIG = -1e30
LOG2E = 1.4426950408889634

ROW_TILE = 512
FRONT_TILE = 1024
FRONT_SUB = 256
FFN_SUB = 256
FFN_COLS = 256
VMEM_LIMIT = 56 * 1024 * 1024


def _spare_lane(head_in_group):
    return HEAD_DIM if head_in_group == 0 else 0


def _rms(x, gain):
    return x * lax.rsqrt(jnp.mean(x * x, axis=-1, keepdims=True) + EPS) * gain


def _ada_kernel(c_ref, w_ref, b_ref, o_ref):
    c = c_ref[...]
    c_act = c / (1.0 + jnp.exp(-c))
    o_ref[0] = jnp.dot(c_act, w_ref[0], preferred_element_type=F32,
                       precision=lax.Precision.HIGHEST) + b_ref[0]


def _ada(c, w_ada, b_ada):
    depth, d, nd = w_ada.shape
    b = c.shape[0]
    return pl.pallas_call(
        _ada_kernel,
        grid=(depth, nd // d),
        in_specs=[
            pl.BlockSpec((b, d), lambda l, j: (0, 0)),
            pl.BlockSpec((1, d, d), lambda l, j: (l, 0, j)),
            pl.BlockSpec((1, 1, d), lambda l, j: (l, 0, j)),
        ],
        out_specs=pl.BlockSpec((1, b, d), lambda l, j: (l, 0, j)),
        out_shape=jax.ShapeDtypeStruct((depth, b, nd), F32),
        name="ada_mod",
    )(c, w_ada, b_ada.reshape(depth, 1, nd))


def _front_kernel(x_ref, mod_ref, nrm_ref, w_in_ref, wpool_ref, pscale_ref, sgn_ref, wcat_ref,
                  sbias_ref, mnp_ref, mng_ref, q_ref, k_ref, v_ref, pg_ref, e_ref, *, tm, sbw, pw):
    s = pl.program_id(1)
    sub = FRONT_SUB
    c0 = 3 * sbw
    heads_per_group = LANES // HEAD_DIM
    gdim = pw // len(POOL_WINDOWS)
    hd = pw // SG_HEADS
    lane_g = lax.broadcasted_iota(jnp.int32, (sub, LANES), 1)
    lane_e = lax.broadcasted_iota(jnp.int32, (sub + POOL_HALO, pw), 1)
    lane = lax.broadcasted_iota(jnp.int32, (sub, pw), 1)
    row = lax.broadcasted_iota(jnp.int32, (sub, pw), 0)
    lane_c = lax.broadcasted_iota(jnp.int32, (CHUNK, pw), 1)
    wlen = jnp.where(lane < gdim, POOL_WINDOWS[0],
                     jnp.where(lane < 2 * gdim, POOL_WINDOWS[1],
                               jnp.where(lane < 3 * gdim, POOL_WINDOWS[2], POOL_WINDOWS[3])))

    @pl.when(s == 0)
    def _():
        e_ref[0:POOL_HALO, :] = jnp.zeros((POOL_HALO, pw), F32)

    @pl.when(s > 0)
    def _():
        e_ref[0:POOL_HALO, :] = e_ref[tm:tm + POOL_HALO, :]

    def normed(r):
        x = x_ref[0, r * sub:(r + 1) * sub, :]
        h = _rms(x, nrm_ref[...]) * (1.0 + mod_ref[0, 1:2, :]) + mod_ref[0, 0:1, :]
        return h.astype(BF16)

    def project(r, hb):
        rows = slice(r * sub, (r + 1) * sub)
        mixer_in = jnp.dot(hb, w_in_ref[:, c0:], preferred_element_type=F32)
        for t, ref in enumerate((q_ref, k_ref, v_ref)):
            y = jnp.dot(hb, w_in_ref[:, t * sbw:(t + 1) * sbw], preferred_element_type=F32)
            for j in range(sbw // LANES):
                grp = y[:, j * LANES:(j + 1) * LANES]
                for hh in range(heads_per_group):
                    own = (lane_g >= hh * HEAD_DIM) & (lane_g < (hh + 1) * HEAD_DIM)
                    fill = jnp.where(lane_g == _spare_lane(hh), 1.0, 0.0) if t == 0 else 0.0
                    ref[0, heads_per_group * j + hh, rows, :] = jnp.where(own, grp, fill).astype(BF16)
        return mixer_in

    def pool(r, p):
        e_ref[POOL_HALO + r * sub:POOL_HALO + (r + 1) * sub, :] = p
        e = e_ref[r * sub:(r + 1) * sub + POOL_HALO, :]
        s2 = e + pltpu.roll(e, 1, 0)
        s4 = s2 + pltpu.roll(s2, 2, 0)
        s8 = s4 + pltpu.roll(s4, 4, 0)
        s16 = s8 + pltpu.roll(s8, 8, 0)
        wsum = jnp.where(lane_e < gdim, s2,
                         jnp.where(lane_e < 2 * gdim, s4, jnp.where(lane_e < 3 * gdim, s8, s16)))
        wsum = wsum[POOL_HALO:, :]
        count = jnp.minimum(s * tm + r * sub + row + 1, wlen).astype(F32)
        pooled = wsum / count - p
        mixed = jnp.dot(pooled.astype(BF16), wpool_ref[...], preferred_element_type=F32) * pscale_ref[...]
        pg_ref[0, r * sub:(r + 1) * sub, 0:pw] = _rms(mixed, mnp_ref[...]).astype(BF16)

    def gate(r, u, vg):
        ug = jax.nn.gelu(u)
        vge = jax.nn.gelu(vg)
        mu = jnp.mean(vge, axis=-1, keepdims=True)
        var = jnp.mean(jnp.square(vge - mu), axis=-1, keepdims=True)
        vn = ((vge - mu) * lax.rsqrt(var + EPS) * sgn_ref[...]).astype(BF16)
        zero = jnp.zeros((CHUNK, pw), BF16)
        outs = []
        for c in range(sub // CHUNK):
            vc = vn[c * CHUNK:(c + 1) * CHUNK, :]
            stacked = jnp.concatenate(
                [jnp.where((lane_c >= hh * hd) & (lane_c < (hh + 1) * hd), vc, zero) for hh in range(SG_HEADS)],
                axis=0)
            outs.append(jnp.dot(wcat_ref[...], stacked, preferred_element_type=F32) + sbias_ref[...])
        g_out = ug * jnp.concatenate(outs, axis=0)
        pg_ref[0, r * sub:(r + 1) * sub, pw:2 * pw] = _rms(g_out, mng_ref[...]).astype(BF16)

    n_sub = tm // sub
    hbs = [normed(r) for r in range(n_sub)]
    for r in range(n_sub):
        mixer_in = project(r, hbs[r])
        pool(r, mixer_in[:, :pw])
        gate(r, mixer_in[:, pw:2 * pw], mixer_in[:, 2 * pw:])


def _front(x, mod, nrm, w_in_b, wpool_bd, pscale, sgn, wcat, sbias, mnp, mng, *, layer, tm, sbw, pw):
    b, s, d = x.shape
    nh = sbw // HEAD_DIM
    in_w = w_in_b.shape[2]
    const = lambda *shape: pl.BlockSpec(shape, lambda bi, si: (0,) * len(shape))
    stacked = lambda *shape: pl.BlockSpec((None,) + shape, lambda bi, si: (layer,) + (0,) * len(shape))
    qkv_spec = pl.BlockSpec((1, nh, tm, LANES), lambda bi, si: (bi, 0, si, 0))
    qkv_shape = jax.ShapeDtypeStruct((b, nh, s, LANES), BF16)
    return pl.pallas_call(
        functools.partial(_front_kernel, tm=tm, sbw=sbw, pw=pw),
        grid=(b, s // tm),
        in_specs=[
            pl.BlockSpec((1, tm, d), lambda bi, si: (bi, si, 0)),
            pl.BlockSpec((1, N_MOD, d), lambda bi, si: (bi, 0, 0)),
            const(1, d), stacked(d, in_w), const(pw, pw), const(1, pw), const(1, pw),
            const(CHUNK, SG_HEADS * CHUNK), const(CHUNK, pw), const(1, pw), const(1, pw),
        ],
        out_specs=[qkv_spec, qkv_spec, qkv_spec,
                   pl.BlockSpec((1, tm, 2 * pw), lambda bi, si: (bi, si, 0))],
        out_shape=[qkv_shape, qkv_shape, qkv_shape, jax.ShapeDtypeStruct((b, s, 2 * pw), BF16)],
        scratch_shapes=[pltpu.VMEM((tm + POOL_HALO, pw), F32)],
        compiler_params=pltpu.CompilerParams(
            dimension_semantics=("arbitrary", "arbitrary"), vmem_limit_bytes=VMEM_LIMIT),
        name="front",
    )(x, mod, nrm, w_in_b, wpool_bd, pscale, sgn, wcat, sbias, mnp, mng)


TRI_WIDTH = 256


def _neg_abs(z):
    sign = jnp.uint32(0x80000000)
    return lax.bitcast_convert_type(lax.bitcast_convert_type(z, jnp.uint32) | sign, F32)


def _mixer_ffn_kernel(q_ref, kp_ref, kc_ref, vp_ref, vc_ref, pg_ref, x_ref, moda_ref, modf_ref, lc_ref, tri_ref,
                      mna_ref, wout_ref, nrm_ref, wg_ref, wu_ref, wd_ref, fn_ref, o_ref, x1_scr, res_scr,
                      *, tq, nh, n_seq, n_tiles, final):
    j = pl.program_id(0)
    i = jnp.minimum(j, n_tiles - 1) % n_seq
    nqb = tq // Q_BLOCK
    edge = KV_SPAN - Q_BLOCK
    group = LANES // HEAD_DIM
    assert KV_SPAN - 2 * Q_BLOCK + Q_BLOCK == 2 * TRI_WIDTH

    @pl.when(j == 0)
    def _():
        x1_scr[...] = jnp.zeros(x1_scr.shape, F32)

    res_scr[...] = x1_scr[...]

    sub = FFN_SUB
    subs = [slice(r * sub, (r + 1) * sub) for r in range(tq // sub)]
    dff = wg_ref.shape[1]
    d_model = wd_ref.shape[1]
    n_ct = dff // FFN_COLS
    hbs, acts = {}, {}

    def ffn_norm(r):
        h = _rms(res_scr[subs[r], :], nrm_ref[...]) * (1.0 + modf_ref[0, 4:5, :]) + modf_ref[0, 3:4, :]
        hbs[r] = h.astype(BF16)

    def ffn_gate_up(r, c):
        cols = slice(c * FFN_COLS, (c + 1) * FFN_COLS)
        gate = jnp.dot(hbs[r], wg_ref[:, cols], preferred_element_type=F32)
        up = jnp.dot(hbs[r], wu_ref[:, cols], preferred_element_type=F32)
        acts[r, c] = (gate / (1.0 + jnp.exp(-gate)) * up).astype(BF16)

    def ffn_down(r, c):
        cols = slice(c * FFN_COLS, (c + 1) * FFN_COLS)
        act = jnp.concatenate([acts[r, k] for k in range(n_ct)], axis=1)
        y = jnp.dot(act, wd_ref[:, cols], preferred_element_type=F32)
        out = res_scr[subs[r], cols] + modf_ref[0, 5:6, cols] * y
        o_ref[0, subs[r], cols] = out

    def ffn_finish(r):
        if final:
            o_ref[0, subs[r], :] = _rms(o_ref[0, subs[r], :], fn_ref[...])

    pieces = []
    for r in range(len(subs)):
        pieces += [functools.partial(ffn_gate_up, r, c) for c in range(n_ct)]
        pieces += [functools.partial(ffn_down, r, c) for c in range(d_model // FFN_COLS)]
        pieces.append(functools.partial(ffn_finish, r))

    kb_prev = jnp.where(i == 0, NEG_BIG, 0.0).astype(BF16)
    prev_bias = [kb_prev * lc_ref[hh:hh + 1, :] for hh in range(group)]

    def window(prev_ref, cur_ref, h, off):
        return prev_ref[0, h, off:, :], cur_ref[0, h, 0:off + Q_BLOCK, :]

    newest = (lax.broadcasted_iota(jnp.int32, (Q_BLOCK, Q_BLOCK), 1)
              < lax.broadcasted_iota(jnp.int32, (Q_BLOCK, Q_BLOCK), 0))
    half = TRI_WIDTH - Q_BLOCK

    def scores(item):
        qb, h = divmod(item, nh)
        off = qb * Q_BLOCK
        k_prev, k_cur = window(kp_ref, kc_ref, h, off)
        keys = jnp.concatenate([k_prev + prev_bias[h % group], k_cur], axis=0)
        z = lax.dot_general(q_ref[0, h, off:off + Q_BLOCK, :], keys, (((1,), (1,)), ((), ())),
                            preferred_element_type=F32)
        return jnp.concatenate([jnp.where(newest, z[:, edge:], z[:, :Q_BLOCK]), z[:, Q_BLOCK:edge]], axis=1)

    def softplus(z):
        n = jnp.maximum(z, 0.0) + jnp.log(1.0 + jnp.exp2(_neg_abs(z))) * LOG2E
        nb = n.astype(BF16)
        n_new = jnp.where(newest, n[:, :Q_BLOCK], 0.0)
        nb_new = n_new.astype(BF16)
        nb_old = nb[:, :Q_BLOCK] - nb_new
        tot_new = jnp.sum(n_new, axis=-1, keepdims=True)
        tot_mid = jnp.sum(n[:, Q_BLOCK + half:], axis=-1, keepdims=True)
        chunks = (jnp.concatenate([nb_old, nb[:, Q_BLOCK:Q_BLOCK + half]], axis=1), nb[:, Q_BLOCK + half:], nb_new)
        return z - n, chunks, (tot_mid + tot_new, tot_new)

    def log_weight(state):
        dmn, (c_first, c_mid, c_new), (tot_first, tot_mid) = state
        l_first = jnp.dot(c_first, tri_ref[...], preferred_element_type=F32) + tot_first
        l_mid = jnp.dot(c_mid, tri_ref[...], preferred_element_type=F32) + tot_mid
        l_new = jnp.dot(c_new, tri_ref[0:Q_BLOCK, 0:Q_BLOCK], preferred_element_type=F32)
        later = jnp.concatenate([jnp.where(newest, l_new, l_first[:, :Q_BLOCK]), l_first[:, Q_BLOCK:], l_mid],
                                axis=1)
        return dmn - later

    def weights(arg):
        w = jnp.exp2(arg)
        wb = w.astype(BF16)
        w_new = jnp.where(newest, w[:, :Q_BLOCK], 0.0).astype(BF16)
        return jnp.concatenate([wb[:, :Q_BLOCK] - w_new, wb[:, Q_BLOCK:], w_new], axis=1)

    def weighted_values(item, w):
        qb, h = divmod(item, nh)
        values = jnp.concatenate(window(vp_ref, vc_ref, h, qb * Q_BLOCK), axis=0)
        return jnp.dot(w, values, preferred_element_type=F32)

    def project(qb, pvs):
        a = jnp.concatenate([sum(pvs[g * group + 1:(g + 1) * group], pvs[g * group])
                             for g in range(nh // group)], axis=1)
        rows = slice(qb * Q_BLOCK, (qb + 1) * Q_BLOCK)
        an = _rms(a, mna_ref[...]).astype(BF16)
        merged = jnp.concatenate([an, pg_ref[0, rows, :]], axis=1)
        y = jnp.dot(merged, wout_ref[...], preferred_element_type=F32)
        x1_scr[rows, :] = x_ref[0, rows, :] + moda_ref[0, 2:3, :] * y

    n_items = nqb * nh
    per_step = -(-len(pieces) // n_items)
    zs, states, ws, pvs = {}, {}, {}, {}
    for step in range(-2, n_items + 2):
        if step == 0:
            for r in range(len(subs)):
                ffn_norm(r)
        for _ in range(per_step):
            if pieces and step >= 0:
                pieces.pop(0)()
        if 0 <= step + 2 < n_items:
            zs[step + 2] = scores(step + 2)
        if 0 <= step - 1 < n_items:
            ws[step - 1] = weights(log_weight(states.pop(step - 1)))
        if 0 <= step < n_items:
            states[step] = softplus(zs.pop(step))
        if 0 <= step - 2 < n_items:
            item = step - 2
            pvs[item] = weighted_values(item, ws.pop(item))
            if item % nh == nh - 1:
                qb = item // nh
                project(qb, [pvs.pop(qb * nh + h) for h in range(nh)])
    assert not pieces


def _mixer_ffn(q, k, v, pg, x, mod, lane_consts, tri, mna, wout_b, nrm, wgu_b, wd_b, fn, *, layer, tq, final):
    b, s, d = x.shape
    nh = q.shape[1]
    mixw = wout_b.shape[1]
    dff = wd_b.shape[1]
    n_seq = s // tq
    n_tiles = b * n_seq
    once = lambda *shape: pl.BlockSpec(shape, lambda j: (0,) * len(shape), pipeline_mode=pl.Buffered(1))
    stacked = lambda *shape, col=0: pl.BlockSpec((None,) + shape, lambda j: (layer, 0, col),
                                                 pipeline_mode=pl.Buffered(1))
    mixer_tile = lambda j: jnp.minimum(j, n_tiles - 1)
    ffn_tile = lambda j: jnp.maximum(j - 1, 0)
    cur = pl.BlockSpec((1, nh, tq, LANES), lambda j: (mixer_tile(j) // n_seq, 0, mixer_tile(j) % n_seq, 0))
    prev = pl.BlockSpec((1, nh, tq, LANES),
                        lambda j: (mixer_tile(j) // n_seq, 0, jnp.maximum(mixer_tile(j) % n_seq - 1, 0), 0))
    rows_a = lambda width: pl.BlockSpec((1, tq, width),
                                        lambda j: (mixer_tile(j) // n_seq, mixer_tile(j) % n_seq, 0))
    return pl.pallas_call(
        functools.partial(_mixer_ffn_kernel, tq=tq, nh=nh, n_seq=n_seq, n_tiles=n_tiles, final=final),
        grid=(n_tiles + 1,),
        in_specs=[
            cur, prev, cur, prev, cur, rows_a(pg.shape[2]), rows_a(d),
            pl.BlockSpec((1, N_MOD, d), lambda j: (mixer_tile(j) // n_seq, 0, 0)),
            pl.BlockSpec((1, N_MOD, d), lambda j: (ffn_tile(j) // n_seq, 0, 0)),
            once(*lane_consts.shape), once(*tri.shape), once(1, nh * HEAD_DIM), stacked(mixw, d),
            once(1, d), stacked(d, dff, col=0), stacked(d, dff, col=1), stacked(dff, d), once(1, d),
        ],
        out_specs=pl.BlockSpec((1, tq, d), lambda j: (ffn_tile(j) // n_seq, ffn_tile(j) % n_seq, 0)),
        out_shape=jax.ShapeDtypeStruct((b, s, d), F32),
        scratch_shapes=[pltpu.VMEM((tq, d), F32), pltpu.VMEM((tq, d), F32)],
        compiler_params=pltpu.CompilerParams(dimension_semantics=("arbitrary",), vmem_limit_bytes=VMEM_LIMIT),
        name="mixer_ffn",
    )(q, k, k, v, v, pg, x, mod, mod, lane_consts, tri, mna, wout_b, nrm, wgu_b, wgu_b, wd_b, fn)


def _lane_constants():
    rows = np.zeros((16, LANES), np.float32)
    for hh in range(LANES // HEAD_DIM):
        rows[hh, _spare_lane(hh)] = 1.0
    return rows


def _suffix_sum_matrix(n):
    j = np.arange(n)
    return (j[:, None] > j[None, :]).astype(np.float32)


def kernel(x, c, w_ada, b_ada, norm_mix_in, w_in, w_pool, pool_scale, sg_norm, w_s, b_s, mix_norm, w_out,
           norm_ffn_in, w_gate_up, w_down, final_norm):
    b, s, d = x.shape
    depth = w_ada.shape[0]
    pw = pool_scale.shape[1]
    sbw = (w_in.shape[2] - 3 * pw) // 3
    tm = min(ROW_TILE, s)
    tf = min(FRONT_TILE, s)
    assert s % tm == 0 and tm % Q_BLOCK == 0 and tm >= SB_WINDOW and sbw % LANES == 0
    assert s % tf == 0 and tf % FRONT_SUB == 0 and FRONT_SUB % CHUNK == 0
    assert tm % FFN_SUB == 0

    mod_all = _ada(c, w_ada, b_ada).reshape(depth, b, N_MOD, d)

    q_scale = jnp.concatenate([jnp.full((sbw,), LOG2E * HEAD_DIM ** -0.5, F32),
                               jnp.ones((w_in.shape[2] - sbw,), F32)])
    w_in_b = (w_in * q_scale).astype(BF16)
    groups = w_pool.shape[1]
    gdim = w_pool.shape[2]
    eye = jnp.eye(groups, dtype=F32)
    wpool_bd = (w_pool[:, :, :, None, :] * eye[None, :, None, :, None]).reshape(
        depth, groups * gdim, groups * gdim).astype(BF16)
    causal = jnp.tril(jnp.ones((CHUNK, CHUNK), dtype=bool))
    ws_c = jnp.where(causal[None, None], w_s, 0.0)
    wcat = ws_c.transpose(0, 2, 1, 3).reshape(depth, CHUNK, SG_HEADS * CHUNK).astype(BF16)
    sbias = jnp.repeat(b_s.transpose(0, 2, 1), pw // SG_HEADS, axis=2)
    w_out_b = w_out.astype(BF16)
    wgu_b = w_gate_up.astype(BF16)
    wd_b = w_down.astype(BF16)
    lane_consts = jnp.asarray(_lane_constants(), dtype=BF16)
    tri = jnp.asarray(_suffix_sum_matrix(TRI_WIDTH), dtype=BF16)
    row = lambda v: v.reshape(1, -1)

    for l in range(depth):
        mod = mod_all[l]
        q, k, v, pg = _front(x, mod, row(norm_mix_in[l]), w_in_b, wpool_bd[l], row(pool_scale[l]),
                             row(sg_norm[l]), wcat[l], sbias[l], row(mix_norm[l, sbw:sbw + pw]),
                             row(mix_norm[l, sbw + pw:]), layer=l, tm=tf, sbw=sbw, pw=pw)
        x = _mixer_ffn(q, k, v, pg, x, mod, lane_consts, tri, row(mix_norm[l, :sbw]), w_out_b,
                       row(norm_ffn_in[l]), wgu_b, wd_b, row(final_norm),
                       layer=l, tq=tm, final=(l == depth - 1))
    return x
```
